```python
import math
import jax, jax.numpy as jnp
from jax import lax
import numpy as np

D_MODEL = 1024
BATCH = 4
SEQ = 4096
DEPTH = 1
DEC_BATCH = 32
DEC_SEQ = 8
PAST_LEN = 16384
PAGE_SIZE = 128

N_HEADS = 8
N_KV_HEADS = 2
HEAD_DIM = 128
GROUP = N_HEADS // N_KV_HEADS
IDX_HEADS = 8
IDX_DIM = 64
TOPK_MAX = 256
Q_BLOCK = 128
D_RNN = D_MODEL
LRU_BLOCKS = 8
LRU_BLOCK_W = D_RNN // LRU_BLOCKS
CONV_W = 4
LRU_C = 8.0
D_FF = ((8 * D_MODEL // 3 + 255) // 256) * 256
N_BUCKETS = 32
MAX_DISTANCE = 128
EPS = 1e-6
NEG_INF = -1e30
SPLIT_SIZES = (N_HEADS * HEAD_DIM, N_KV_HEADS * HEAD_DIM, N_KV_HEADS * HEAD_DIM,
               IDX_HEADS * IDX_DIM, IDX_DIM, IDX_HEADS, D_RNN, D_MODEL, D_MODEL)
D_IN = sum(SPLIT_SIZES)

kernel_name = 'dsa_rglru_gated_hybrid'


def rms_norm(x, g):
    xf = x.astype(jnp.float32)
    y = xf * lax.rsqrt(jnp.mean(xf * xf, axis=-1, keepdims=True) + EPS)
    return (y * g.astype(jnp.float32)).astype(x.dtype)


def project(xn, w_in):
    B, T = xn.shape[:2]
    z = jnp.einsum('btd,de->bte', xn, w_in)
    pts, acc = [], 0
    for s in SPLIT_SIZES[:-1]:
        acc += s
        pts.append(acc)
    q, k, v, qi, ki, wi, u, ga, gb = jnp.split(z, pts, axis=-1)
    q = q.reshape(B, T, N_HEADS, HEAD_DIM)
    k = k.reshape(B, T, N_KV_HEADS, HEAD_DIM)
    v = v.reshape(B, T, N_KV_HEADS, HEAD_DIM)
    qi = qi.reshape(B, T, IDX_HEADS, IDX_DIM)
    wi = wi * (IDX_HEADS ** -0.5 * IDX_DIM ** -0.5)
    return q, k, v, qi, ki, wi, u, ga, gb


def rel_bucket(dist):
    max_exact = N_BUCKETS // 2
    d = jnp.maximum(dist, 0)
    df = jnp.maximum(d, 1).astype(jnp.float32)
    large = max_exact + (jnp.log(df / max_exact) / math.log(MAX_DISTANCE / max_exact)
                         * (N_BUCKETS - max_exact)).astype(jnp.int32)
    large = jnp.minimum(large, N_BUCKETS - 1)
    return jnp.where(d < max_exact, d, large)


def index_topk(qi, wi, ki, qpos, topk):
    s = jax.nn.relu(jnp.einsum('bqhd,bld->bqhl', qi, ki).astype(jnp.float32))
    score = jnp.einsum('bqhl,bqh->bql', s, wi.astype(jnp.float32))
    kpos = jnp.arange(ki.shape[1], dtype=jnp.int32)
    score = jnp.where(kpos[None, None, :] <= qpos[None, :, None], score, NEG_INF)
    _, sel = lax.top_k(score, topk)
    return sel.astype(jnp.int32)


def gather_rows(rows, sel):
    return jax.vmap(lambda r, s: r[s])(rows, sel)


def sparse_attend(q, qpos, sel, k_sel, v_sel, rel_bias):
    B, Q = q.shape[:2]
    qg = q.reshape(B, Q, N_KV_HEADS, GROUP, HEAD_DIM)
    logits = jnp.einsum('bqngd,bqknd->bqngk', qg, k_sel).astype(jnp.float32) * (HEAD_DIM ** -0.5)
    dist = qpos[None, :, None] - sel
    valid = dist >= 0
    bias = rel_bias[rel_bucket(dist)].astype(jnp.float32)
    bias = jnp.moveaxis(bias.reshape(B, Q, -1, N_KV_HEADS, GROUP), 2, -1)
    logits = jnp.where(valid[:, :, None, None, :], logits + bias, NEG_INF)
    p = jax.nn.softmax(logits, axis=-1).astype(v_sel.dtype)
    out = jnp.einsum('bqngk,bqknd->bqngd', p, v_sel)
    return out.reshape(B, Q, N_HEADS * HEAD_DIM)


def prompt_attention(q, k, v, qi, ki, wi, rel_bias):
    B, T = q.shape[:2]
    topk = min(TOPK_MAX, T // 4)
    nb = T // Q_BLOCK
    pos = jnp.arange(T, dtype=jnp.int32)

    def block(args):
        qb, qib, wib, pb = args
        sel = index_topk(qib, wib, ki, pb, topk)
        return sparse_attend(qb, pb, sel, gather_rows(k, sel), gather_rows(v, sel), rel_bias)

    def to_blocks(a):
        return a.reshape((B, nb, Q_BLOCK) + a.shape[2:]).swapaxes(0, 1)

    out = lax.map(block, (to_blocks(q), to_blocks(qi), to_blocks(wi), pos.reshape(nb, Q_BLOCK)))
    return out.swapaxes(0, 1).reshape(B, T, N_HEADS * HEAD_DIM)


def sample_attention(q, k_new, v_new, qi, ki_new, wi, cache_k, cache_v, cache_kidx, layer,
                     page_table, rel_bias):
    DB, S = q.shape[:2]
    n_pages = page_table.shape[1]
    past = n_pages * PAGE_SIZE
    topk = min(TOPK_MAX, (past + S) // 4)
    ki_past = cache_kidx[layer, page_table].reshape(DB, past, IDX_DIM)
    ki_all = jnp.concatenate([ki_past.astype(ki_new.dtype), ki_new], axis=1)
    qpos = past + jnp.arange(S, dtype=jnp.int32)
    sel = index_topk(qi, wi, ki_all, qpos, topk)
    from_past = (sel < past)[..., None, None]
    sp = jnp.minimum(sel, past - 1)
    phys = jax.vmap(lambda pt, s: pt[s // PAGE_SIZE])(page_table, sp)
    off = sp % PAGE_SIZE
    sn = jnp.clip(sel - past, 0, S - 1)
    k_sel = jnp.where(from_past, cache_k[layer, phys, off], gather_rows(k_new, sn))
    v_sel = jnp.where(from_past, cache_v[layer, phys, off], gather_rows(v_new, sn))
    return sparse_attend(q, qpos, sel, k_sel, v_sel, rel_bias)


def rglru(u, conv_buf, h0, conv_w, conv_b, w_rg, b_rg, w_ig, b_ig, lam):
    B, T = u.shape[:2]
    ext = jnp.concatenate([conv_buf.astype(u.dtype), u], axis=1)
    xc = conv_b + sum(ext[:, j:j + T] * conv_w[j] for j in range(CONV_W))
    new_buf = ext[:, T:]
    xb = xc.reshape(B, T, LRU_BLOCKS, LRU_BLOCK_W)
    r = jax.nn.sigmoid(jnp.einsum('btnd,nde->btne', xb, w_rg) + b_rg).reshape(B, T, D_RNN)
    i = jax.nn.sigmoid(jnp.einsum('btnd,nde->btne', xb, w_ig) + b_ig).reshape(B, T, D_RNN)
    log_a = -LRU_C * r.astype(jnp.float32) * jax.nn.softplus(-lam.astype(jnp.float32))
    a = jnp.exp(log_a)
    bt = jnp.sqrt(-jnp.expm1(2.0 * log_a)) * (i * xc).astype(jnp.float32)

    def step(h, ab):
        a_t, b_t = ab
        h = a_t * h + b_t
        return h, h

    h_t, hs = lax.scan(step, h0.astype(jnp.float32), (a.swapaxes(0, 1), bt.swapaxes(0, 1)))
    return hs.swapaxes(0, 1).astype(u.dtype), new_buf, h_t


def merge_ffn(x, attn_out, lru_out, ga, gb, w_o_attn, w_o_lru, w_out, g_ffn, w_fg, w_fu, w_fd):
    merged = jax.nn.sigmoid(ga) * (attn_out @ w_o_attn) + jax.nn.sigmoid(gb) * (lru_out @ w_o_lru)
    h = x + merged @ w_out
    hn = rms_norm(h, g_ffn)
    return h + (jax.nn.silu(hn @ w_fg) * (hn @ w_fu)) @ w_fd


def setup_inputs(seed: int = 0) -> dict:
    key = jax.random.key(seed)
    ks = jax.random.split(key, 32)
    f32 = jnp.float32

    def nrm(k, shape, fan_in):
        return jax.random.normal(k, shape, f32) * fan_in ** -0.5

    n_pages = PAST_LEN // PAGE_SIZE
    used = DEC_BATCH * n_pages
    n_pool = used + max(1, used // 4)
    perm = jax.random.permutation(ks[0], n_pool)
    page_table = perm[:used].reshape(DEC_BATCH, n_pages).astype(jnp.int32)
    u = jax.random.uniform(ks[1], (DEPTH, D_RNN), f32, minval=0.9, maxval=0.999)
    a0 = u ** (1.0 / LRU_C)
    lru_lambda = jnp.log(a0) - jnp.log1p(-a0)
    return {
        'x_prompt': jax.random.normal(ks[2], (BATCH, SEQ, D_MODEL), f32),
        'x_sample': jax.random.normal(ks[3], (DEC_BATCH, DEC_SEQ, D_MODEL), f32),
        'cache_k': jax.random.normal(ks[4], (DEPTH, n_pool, PAGE_SIZE, N_KV_HEADS, HEAD_DIM), f32),
        'cache_v': jax.random.normal(ks[5], (DEPTH, n_pool, PAGE_SIZE, N_KV_HEADS, HEAD_DIM), f32),
        'cache_kidx': jax.random.normal(ks[6], (DEPTH, n_pool, PAGE_SIZE, IDX_DIM), f32),
        'state_conv': jax.random.normal(ks[7], (DEPTH, DEC_BATCH, CONV_W - 1, D_RNN), f32),
        'state_rnn': 0.5 * jax.random.normal(ks[8], (DEPTH, DEC_BATCH, D_RNN), f32),
        'page_table': page_table,
        'rel_bias': 0.5 * jax.random.normal(ks[9], (N_BUCKETS, N_HEADS), f32),
        'g_mix': 1.0 + 0.01 * jax.random.normal(ks[10], (DEPTH, D_MODEL), f32),
        'w_in': nrm(ks[11], (DEPTH, D_MODEL, D_IN), D_MODEL),
        'conv_w': nrm(ks[12], (DEPTH, CONV_W, D_RNN), CONV_W),
        'conv_b': 0.01 * jax.random.normal(ks[13], (DEPTH, D_RNN), f32),
        'w_rgate': nrm(ks[14], (DEPTH, LRU_BLOCKS, LRU_BLOCK_W, LRU_BLOCK_W), LRU_BLOCK_W),
        'b_rgate': 0.01 * jax.random.normal(ks[15], (DEPTH, LRU_BLOCKS, LRU_BLOCK_W), f32),
        'w_igate': nrm(ks[16], (DEPTH, LRU_BLOCKS, LRU_BLOCK_W, LRU_BLOCK_W), LRU_BLOCK_W),
        'b_igate': 0.01 * jax.random.normal(ks[17], (DEPTH, LRU_BLOCKS, LRU_BLOCK_W), f32),
        'lru_lambda': lru_lambda,
        'w_o_attn': nrm(ks[18], (DEPTH, N_HEADS * HEAD_DIM, D_MODEL), N_HEADS * HEAD_DIM),
        'w_o_lru': nrm(ks[19], (DEPTH, D_RNN, D_MODEL), D_RNN),
        'w_out': nrm(ks[20], (DEPTH, D_MODEL, D_MODEL), D_MODEL),
        'g_ffn': 1.0 + 0.01 * jax.random.normal(ks[21], (DEPTH, D_MODEL), f32),
        'w_ffn_gate': nrm(ks[22], (DEPTH, D_MODEL, D_FF), D_MODEL),
        'w_ffn_up': nrm(ks[23], (DEPTH, D_MODEL, D_FF), D_MODEL),
        'w_ffn_down': nrm(ks[24], (DEPTH, D_FF, D_MODEL), D_FF),
        'g_final': 1.0 + 0.01 * jax.random.normal(ks[25], (D_MODEL,), f32),
    }


def reference(x_prompt, x_sample, cache_k, cache_v, cache_kidx, state_conv, state_rnn, page_table,
              rel_bias, g_mix, w_in, conv_w, conv_b, w_rgate, b_rgate, w_igate, b_igate, lru_lambda,
              w_o_attn, w_o_lru, w_out, g_ffn, w_ffn_gate, w_ffn_up, w_ffn_down, g_final):
    B = x_prompt.shape[0]
    xp, xs = x_prompt, x_sample
    kp, vp, kip, cp, hp = [], [], [], [], []
    ks_, vs_, kis, cs, hs = [], [], [], [], []
    for l in range(DEPTH):
        lru_w = (conv_w[l], conv_b[l], w_rgate[l], b_rgate[l], w_igate[l], b_igate[l], lru_lambda[l])
        out_w = (w_o_attn[l], w_o_lru[l], w_out[l], g_ffn[l], w_ffn_gate[l], w_ffn_up[l], w_ffn_down[l])
        q, k, v, qi, ki, wi, u, ga, gb = project(rms_norm(xp, g_mix[l]), w_in[l])
        attn = prompt_attention(q, k, v, qi, ki, wi, rel_bias)
        buf0 = jnp.zeros((B, CONV_W - 1, D_RNN), u.dtype)
        h0 = jnp.zeros((B, D_RNN), jnp.float32)
        lru, buf_p, h_p = rglru(u, buf0, h0, *lru_w)
        xp = merge_ffn(xp, attn, lru, ga, gb, *out_w)
        kp.append(k); vp.append(v); kip.append(ki); cp.append(buf_p); hp.append(h_p)
        q, k, v, qi, ki, wi, u, ga, gb = project(rms_norm(xs, g_mix[l]), w_in[l])
        attn = sample_attention(q, k, v, qi, ki, wi, cache_k, cache_v, cache_kidx, l,
                                page_table, rel_bias)
        lru, buf_s, h_s = rglru(u, state_conv[l], state_rnn[l], *lru_w)
        xs = merge_ffn(xs, attn, lru, ga, gb, *out_w)
        ks_.append(k); vs_.append(v); kis.append(ki); cs.append(buf_s); hs.append(h_s)
    y_prompt = rms_norm(xp, g_final)
    y_sample = rms_norm(xs, g_final)
    new_k_prompt = jnp.stack(kp)
    new_v_prompt = jnp.stack(vp)
    new_kidx_prompt = jnp.stack(kip)
    new_conv_prompt = jnp.stack(cp)
    new_rnn_prompt = jnp.stack(hp)
    new_k_sample = jnp.stack(ks_)
    new_v_sample = jnp.stack(vs_)
    new_kidx_sample = jnp.stack(kis)
    new_conv_sample = jnp.stack(cs)
    new_rnn_sample = jnp.stack(hs)
    return (y_prompt, y_sample, new_k_prompt, new_v_prompt, new_kidx_prompt, new_conv_prompt,
            new_rnn_prompt, new_k_sample, new_v_sample, new_kidx_sample, new_conv_sample,
            new_rnn_sample)
```

```python
import functools
import math

import jax
import jax.numpy as jnp
from jax import lax
from jax.experimental import pallas as pl
from jax.experimental.pallas import tpu as pltpu

D_MODEL = 1024
N_HEADS = 8
N_KV_HEADS = 2
HEAD_DIM = 128
GROUP = N_HEADS // N_KV_HEADS
IDX_HEADS = 8
IDX_DIM = 64
TOPK_MAX = 256
D_RNN = D_MODEL
LRU_BLOCKS = 8
LRU_BLOCK_W = D_RNN // LRU_BLOCKS
CONV_W = 4
LRU_C = 8.0
N_BUCKETS = 32
MAX_EXACT = N_BUCKETS // 2
MAX_DISTANCE = 128
EPS = 1e-6
NEG_INF = -1e30
PAGE_SIZE = 128
KV_DIM = N_KV_HEADS * HEAD_DIM
QK_SCALE = HEAD_DIM ** -0.5
WI_SCALE = IDX_HEADS ** -0.5 * IDX_DIM ** -0.5

LANES = 128
SUBLANES = 8
VMEM_LIMIT = 56 * 1024 * 1024

_BUCKET_STEPS = tuple(
    math.ceil(MAX_EXACT * (MAX_DISTANCE / MAX_EXACT) ** (k / (N_BUCKETS - MAX_EXACT)))
    for k in range(1, N_BUCKETS - MAX_EXACT))
M_INIT = -1e29

_INT_MIN = -2 ** 31
bf16 = jnp.bfloat16
f32 = jnp.float32
i32 = jnp.int32


def _dot(a, b):
    return jnp.dot(a, b, preferred_element_type=f32)


def _dot_nt(a, b):
    return lax.dot_general(a, b, (((1,), (1,)), ((), ())), preferred_element_type=f32)


def _const_spec(shape):
    zeros = (0,) * len(shape)
    return pl.BlockSpec(shape, lambda *_: zeros, pipeline_mode=pl.Buffered(1))


def _smem_spec():
    return pl.BlockSpec(memory_space=pltpu.SMEM)


def _sortable(s):
    bits = lax.bitcast_convert_type(s, i32)
    return jnp.where(bits < 0, -(bits & 0x7FFFFFFF), bits)


def _bucket(dist):
    d = jnp.maximum(dist, 0)
    large = jnp.full(d.shape, MAX_EXACT, i32)
    for step in _BUCKET_STEPS:
        large = large + (d >= step).astype(i32)
    return jnp.where(d < MAX_EXACT, d, large)


def _proj_kernel(x_ref, g_ref, wq_ref, wkv_ref, wqi_ref, wke_ref, wko_ref, wkw_ref, wu_ref,
                 wga_ref, wgb_ref,
                 q_ref, k_ref, v_ref, kb_ref, vb_ref, qi_ref, kie_ref, kio_ref, kw_ref,
                 u_ref, ga_ref, gb_ref):
    x = x_ref[...]
    ms = jnp.mean(x * x, axis=-1, keepdims=True)
    xn = ((x * lax.rsqrt(ms + EPS)) * g_ref[...]).astype(bf16)

    q_ref[...] = _dot(xn, wq_ref[...]).astype(bf16)
    kv = _dot(xn, wkv_ref[...])
    k_ref[...] = kv[:, :KV_DIM]
    v_ref[...] = kv[:, KV_DIM:]
    kb_ref[...] = kv[:, :KV_DIM].astype(bf16)
    vb_ref[...] = kv[:, KV_DIM:].astype(bf16)
    qi_ref[...] = _dot(xn, wqi_ref[...]).astype(bf16)
    kie_ref[...] = _dot(xn, wke_ref[...]).astype(bf16)
    kio_ref[...] = _dot(xn, wko_ref[...]).astype(bf16)
    kw = _dot(xn, wkw_ref[...])
    lane = lax.broadcasted_iota(i32, kw.shape, 1)
    kw_ref[...] = jnp.where(lane >= IDX_DIM, kw * WI_SCALE, kw)
    u_ref[...] = _dot(xn, wu_ref[...])
    ga_ref[...] = _dot(xn, wga_ref[...])
    gb_ref[...] = _dot(xn, wgb_ref[...])


def _split_w_in(w_in):
    sizes = (N_HEADS * HEAD_DIM, KV_DIM, KV_DIM, IDX_HEADS * IDX_DIM, IDX_DIM, IDX_HEADS,
             D_RNN, D_MODEL, D_MODEL)
    parts, acc = [], 0
    for s in sizes:
        parts.append(w_in[:, acc:acc + s])
        acc += s
    wq, wk, wv, wqi, wki, wwi, wu, wga, wgb = parts
    zk = jnp.zeros_like(wki)
    wkv = jnp.concatenate([wk, wv], axis=1)
    wke = jnp.concatenate([wki, zk], axis=1)
    wko = jnp.concatenate([zk, wki], axis=1)
    wkw = jnp.concatenate(
        [wki, wwi, jnp.zeros((w_in.shape[0], LANES - IDX_DIM - IDX_HEADS), w_in.dtype)], axis=1)
    return tuple(w.astype(bf16) for w in (wq, wkv, wqi, wke, wko, wkw, wu, wga, wgb))


def _proj(x2, g_mix, w_parts):
    n = x2.shape[0]
    tm = min(256, n)
    assert n % tm == 0

    def rows(width):
        return pl.BlockSpec((tm, width), lambda i: (i, 0))

    out_widths = (N_HEADS * HEAD_DIM, KV_DIM, KV_DIM, KV_DIM, KV_DIM, IDX_HEADS * IDX_DIM,
                  LANES, LANES, LANES, D_RNN, D_MODEL, D_MODEL)
    out_dtypes = (bf16, f32, f32, bf16, bf16, bf16, bf16, bf16, f32, f32, f32, f32)
    return pl.pallas_call(
        _proj_kernel,
        grid=(n // tm,),
        in_specs=[rows(D_MODEL), _const_spec((1, D_MODEL))] + [_const_spec(w.shape) for w in w_parts],
        out_specs=[rows(w) for w in out_widths],
        out_shape=[jax.ShapeDtypeStruct((n, w), d) for w, d in zip(out_widths, out_dtypes)],
        compiler_params=pltpu.CompilerParams(
            dimension_semantics=("arbitrary",), vmem_limit_bytes=VMEM_LIMIT),
        name="proj",
    )(x2, g_mix.reshape(1, D_MODEL), *w_parts)


def _kth_largest(count_ge, rows, k):
    def body(it, t_u):
        bit = lax.shift_left(jnp.int32(1), jnp.int32(31) - it)
        cand_u = t_u | bit
        cnt = count_ge(cand_u ^ _INT_MIN)
        return jnp.where(cnt >= k, cand_u, t_u)

    t_u = lax.fori_loop(0, 32, body, jnp.zeros((rows, 1), i32))
    return t_u ^ _INT_MIN


def _tie_cut(count_eq_below, rows, need, n_bits):
    def body(it, x):
        bit = lax.shift_left(jnp.int32(1), jnp.int32(n_bits - 1) - it)
        cand = x | bit
        cnt = count_eq_below(cand)
        return jnp.where(cnt < need, cand, x)

    return lax.fori_loop(0, n_bits, body, jnp.zeros((rows, 1), i32))


def _prompt_attn_kernel(rb_ref, q_ref, qi_ref, kw_ref, kb_ref, vb_ref, kie_ref, kio_ref, o_ref,
                        key_ref, bias_ref, m_ref, l_ref, acc_ref, *, tq, topk, seq):
    b = pl.program_id(0)
    i = pl.program_id(1)
    n_slab = tq // LANES
    row = lax.broadcasted_iota(i32, (tq, tq), 0)
    col = lax.broadcasted_iota(i32, (tq, tq), 1)

    @pl.when((b == 0) & (i == 0))
    def _():
        for t in range(2):
            bucket = _bucket(t * tq + row - col)
            for h in range(N_HEADS):
                bias_ref[t * N_HEADS + h] = jnp.zeros((tq, tq), f32)

            def fill(bb, carry, t=t, bucket=bucket):
                hit = bucket == bb
                for h in range(N_HEADS):
                    bias_ref[t * N_HEADS + h] = jnp.where(hit, rb_ref[bb, h], bias_ref[t * N_HEADS + h])
                return carry

            lax.fori_loop(0, N_BUCKETS, fill, 0)

    w = kw_ref[...]

    def score_block(j, causal):
        off = pl.multiple_of(j * tq, tq)
        ke = kie_ref[pl.ds(off, tq), :]
        ko = kio_ref[pl.ds(off, tq), :]
        s = jnp.zeros((tq, tq), f32)
        for p in range(IDX_HEADS // 2):
            lhs = qi_ref[:, p * LANES:(p + 1) * LANES]
            c0 = IDX_DIM + 2 * p
            s = s + jnp.maximum(_dot_nt(lhs, ke), 0.0) * w[:, c0:c0 + 1]
            s = s + jnp.maximum(_dot_nt(lhs, ko), 0.0) * w[:, c0 + 1:c0 + 2]
        if causal:
            s = jnp.where(col <= row, s, NEG_INF)
        key_ref[:, pl.ds(off, tq)] = _sortable(s)

    def score_loop(j, carry):
        score_block(j, False)
        return carry

    lax.fori_loop(0, i, score_loop, 0)
    score_block(i, True)

    def lane_count(pred_of_slab):
        def body(j, c):
            off = pl.multiple_of(j * tq, tq)
            for s_ in range(n_slab):
                kk = key_ref[:, pl.ds(off + s_ * LANES, LANES)]
                c = c + pred_of_slab(kk, off + s_ * LANES)
            return c

        c = lax.fori_loop(0, i + 1, body, jnp.zeros((tq, LANES), i32))
        return jnp.sum(c, axis=-1, keepdims=True)

    def count_ge(cand):
        return lane_count(lambda kk, off: (kk >= cand).astype(i32))

    thr = _kth_largest(count_ge, tq, topk)

    n_ge = count_ge(thr)

    @pl.when(jnp.max(n_ge) > topk)
    def _():
        need = topk - count_ge(thr + 1)
        lane = lax.broadcasted_iota(i32, (tq, LANES), 1)

        def count_eq_below(x):
            return lane_count(lambda kk, off: ((kk == thr) & (lane + off < x)).astype(i32))

        cut = _tie_cut(count_eq_below, tq, need, (seq - 1).bit_length())

        def demote(j, carry):
            off = pl.multiple_of(j * tq, tq)
            for s_ in range(n_slab):
                sl = pl.ds(off + s_ * LANES, LANES)
                kk = key_ref[:, sl]
                key_ref[:, sl] = jnp.where((kk == thr) & (lane + (off + s_ * LANES) > cut), thr - 1, kk)
            return carry

        lax.fori_loop(0, i + 1, demote, 0)

    m_ref[...] = jnp.full(m_ref.shape, M_INIT, f32)
    l_ref[...] = jnp.zeros(l_ref.shape, f32)
    acc_ref[...] = jnp.zeros(acc_ref.shape, f32)

    def attend_block(j, mode):
        off = pl.multiple_of(j * tq, tq)
        sel = key_ref[:, pl.ds(off, tq)] >= thr
        if mode == 2:
            sel = sel & (col <= row)
        for h in range(N_HEADS):
            n = h // GROUP
            hs = slice(h * HEAD_DIM, (h + 1) * HEAD_DIM)
            ns = slice(n * HEAD_DIM, (n + 1) * HEAD_DIM)
            s = _dot_nt(q_ref[:, hs], kb_ref[pl.ds(off, tq), ns]) * QK_SCALE
            if mode == 0:
                s = s + rb_ref[N_BUCKETS - 1, h]
            else:
                s = s + bias_ref[(2 - mode) * N_HEADS + h]
            s = jnp.where(sel, s, NEG_INF)
            m_old = m_ref[h]
            m_new = jnp.maximum(m_old, jnp.max(s, axis=-1, keepdims=True))
            alpha = jnp.exp(m_old - m_new)
            p = jnp.exp(s - m_new)
            l_ref[h] = alpha * l_ref[h] + jnp.sum(p, axis=-1, keepdims=True)
            acc_ref[:, hs] = alpha * acc_ref[:, hs] + _dot(p.astype(bf16), vb_ref[pl.ds(off, tq), ns])
            m_ref[h] = m_new

    def far_loop(j, carry):
        attend_block(j, 0)
        return carry

    lax.fori_loop(0, i - 1, far_loop, 0)

    @pl.when(i >= 1)
    def _():
        attend_block(i - 1, 1)

    attend_block(i, 2)

    for h in range(N_HEADS):
        hs = slice(h * HEAD_DIM, (h + 1) * HEAD_DIM)
        o_ref[:, hs] = (acc_ref[:, hs] / l_ref[h]).astype(o_ref.dtype)


def _prompt_attn(rel_bias, q, qi, kw, kb, vb, kie, kio, batch, seq):
    tq = min(256, seq)
    assert seq % tq == 0 and tq % LANES == 0 and tq >= MAX_DISTANCE
    nq = seq // tq
    topk = min(TOPK_MAX, seq // 4)

    def qrows(width):
        return pl.BlockSpec((tq, width), lambda b, i: (b * nq + i, 0))

    def krows(width):
        return pl.BlockSpec((seq, width), lambda b, i: (b, 0))

    kern = functools.partial(_prompt_attn_kernel, tq=tq, topk=topk, seq=seq)
    return pl.pallas_call(
        kern,
        grid=(batch, nq),
        in_specs=[_smem_spec(), qrows(N_HEADS * HEAD_DIM), qrows(IDX_HEADS * IDX_DIM), qrows(LANES),
                  krows(KV_DIM), krows(KV_DIM), krows(LANES), krows(LANES)],
        out_specs=qrows(N_HEADS * HEAD_DIM),
        out_shape=jax.ShapeDtypeStruct((batch * seq, N_HEADS * HEAD_DIM), bf16),
        scratch_shapes=[
            pltpu.VMEM((tq, seq), i32),
            pltpu.VMEM((2 * N_HEADS, tq, tq), f32),
            pltpu.VMEM((N_HEADS, tq, 1), f32),
            pltpu.VMEM((N_HEADS, tq, 1), f32),
            pltpu.VMEM((tq, N_HEADS * HEAD_DIM), f32),
        ],
        compiler_params=pltpu.CompilerParams(
            dimension_semantics=("arbitrary", "arbitrary"), vmem_limit_bytes=VMEM_LIMIT),
        name="prompt_attn",
    )(rel_bias, q, qi, kw, kb, vb, kie, kio)


def _sample_index_kernel(pt_ref, qh_ref, wc_ref, kin_ref, *rest, pages_per_step, n_new):
    del pt_ref
    page_refs = rest[:pages_per_step]
    past_ref, new_ref = rest[pages_per_step:]
    c = pl.program_id(1)
    qh = qh_ref[...]
    wc = wc_ref[...]

    def scores(keys):
        s = jnp.maximum(_dot_nt(qh, keys), 0.0) * wc
        out = s[0:n_new]
        for h in range(1, IDX_HEADS):
            out = out + s[h * n_new:(h + 1) * n_new]
        return out

    for pp in range(pages_per_step):
        sc = scores(page_refs[pp][...].astype(bf16))
        past_ref[:, pp * PAGE_SIZE:(pp + 1) * PAGE_SIZE] = _sortable(sc)

    @pl.when(c == 0)
    def _():
        sc = scores(kin_ref[...])
        tok = lax.broadcasted_iota(i32, sc.shape, 0)
        key = lax.broadcasted_iota(i32, sc.shape, 1)
        new_ref[...] = _sortable(jnp.where(key <= tok, sc, NEG_INF))


def _sample_attn_kernel(pt_ref, rb_ref, past_ref, new_ref, q_ref, kn_ref, vn_ref, *rest,
                        pages_per_step, n_new, n_past, topk):
    del pt_ref
    P = pages_per_step
    k_refs = rest[:P]
    v_refs = rest[P:2 * P]
    o_ref = rest[2 * P]
    key_ref, s_ref, thr_ref, m_ref, l_ref, acc_ref = rest[2 * P + 1:]
    c = pl.program_id(1)
    n_steps = pl.num_programs(1)
    n_rows = N_HEADS * n_new
    page_cols = N_KV_HEADS * PAGE_SIZE
    n_keys = n_past + LANES

    def head_column(bucket_row):
        r = lax.broadcasted_iota(i32, (n_rows, 1), 0) // n_new
        out = jnp.zeros((n_rows, 1), f32)
        for h in range(N_HEADS):
            out = jnp.where(r == h, rb_ref[bucket_row, h], out)
        return out

    def bias_tile(dist):
        bucket = _bucket(dist)

        def fill(bb, acc):
            return jnp.where(bucket == bb, head_column(bb), acc)

        return lax.fori_loop(0, N_BUCKETS, fill, jnp.zeros(dist.shape, f32))

    @pl.when(c == 0)
    def _():
        key_ref[:, :n_past] = past_ref[...]
        key_ref[:, n_past:] = new_ref[...]

        def count(pred):
            return jnp.sum(pred(key_ref[...]).astype(i32), axis=-1, keepdims=True)

        def count_ge(cand):
            return count(lambda kk: kk >= cand)

        thr = _kth_largest(count_ge, n_new, topk)
        thr_ref[...] = thr

        @pl.when(jnp.max(count_ge(thr)) > topk)
        def _():
            need = topk - count_ge(thr + 1)
            colk = lax.broadcasted_iota(i32, (n_new, n_keys), 1)
            cut = _tie_cut(lambda x: count(lambda kk: (kk == thr) & (colk < x)),
                           n_new, need, (n_keys - 1).bit_length())
            kk = key_ref[...]
            key_ref[...] = jnp.where((kk == thr) & (colk > cut), thr - 1, kk)

        m_ref[...] = jnp.full(m_ref.shape, M_INIT, f32)
        l_ref[...] = jnp.zeros(l_ref.shape, f32)
        acc_ref[...] = jnp.zeros(acc_ref.shape, f32)

    thr = thr_ref[...]
    q = q_ref[...]
    row_kv = lax.broadcasted_iota(i32, (n_rows, page_cols), 0) // (GROUP * n_new)
    col_kv = lax.broadcasted_iota(i32, (n_rows, page_cols), 1) % N_KV_HEADS
    own_head = row_kv == col_kv
    expand = (lax.broadcasted_iota(i32, (PAGE_SIZE, page_cols), 1) // N_KV_HEADS
              == lax.broadcasted_iota(i32, (PAGE_SIZE, page_cols), 0)).astype(bf16)
    far_bias = head_column(N_BUCKETS - 1)

    def online_update(s, pv):
        m_old = m_ref[...]
        m_new = jnp.maximum(m_old, jnp.max(s, axis=-1, keepdims=True))
        alpha = jnp.exp(m_old - m_new)
        p = jnp.exp(s - m_new)
        l_ref[...] = alpha * l_ref[...] + jnp.sum(p, axis=-1, keepdims=True)
        acc_ref[...] = alpha * acc_ref[...] + pv(p.astype(bf16))
        m_ref[...] = m_new

    for pp in range(P):
        base = pl.multiple_of((c * P + pp) * PAGE_SIZE, PAGE_SIZE)
        sel = (key_ref[:, pl.ds(base, PAGE_SIZE)] >= thr).astype(f32)
        sel = jnp.concatenate([sel] * N_HEADS, axis=0).astype(bf16)
        sel = (_dot(sel, expand) > 0.5) & own_head
        s = _dot_nt(q, k_refs[pp][...].astype(bf16)) * QK_SCALE
        if pp == P - 1:
            tok = lax.broadcasted_iota(i32, (n_rows, page_cols), 0) % n_new
            kpos = lax.broadcasted_iota(i32, (n_rows, page_cols), 1) // N_KV_HEADS
            bias = lax.cond(c == n_steps - 1,
                            lambda: bias_tile(PAGE_SIZE + tok - kpos),
                            lambda: jnp.broadcast_to(far_bias, (n_rows, page_cols)))
            s = s + bias
        else:
            s = s + far_bias
        s_ref[:, pp * page_cols:(pp + 1) * page_cols] = jnp.where(sel, s, NEG_INF)

    def pv_pages(p):
        out = jnp.zeros((n_rows, HEAD_DIM), f32)
        for pp in range(P):
            out = out + _dot(p[:, pp * page_cols:(pp + 1) * page_cols], v_refs[pp][...].astype(bf16))
        return out

    online_update(s_ref[...], pv_pages)

    @pl.when(c == n_steps - 1)
    def _():
        tok = lax.broadcasted_iota(i32, (n_rows, LANES), 0) % n_new
        kpos = lax.broadcasted_iota(i32, (n_rows, LANES), 1)
        sel = jnp.concatenate([(key_ref[:, n_past:] >= thr).astype(i32)] * N_HEADS, axis=0)
        sel = (sel > 0) & (kpos <= tok)
        bias = bias_tile(tok - kpos)
        kn = kn_ref[...]
        vn = vn_ref[...]
        half = GROUP * n_new
        s = jnp.concatenate(
            [_dot_nt(q[n * half:(n + 1) * half], kn[:, n * HEAD_DIM:(n + 1) * HEAD_DIM])
             for n in range(N_KV_HEADS)], axis=0) * QK_SCALE + bias
        s = jnp.where(sel, s, NEG_INF)

        def pv_new(p):
            return jnp.concatenate(
                [_dot(p[n * half:(n + 1) * half], vn[:, n * HEAD_DIM:(n + 1) * HEAD_DIM])
                 for n in range(N_KV_HEADS)], axis=0)

        online_update(s, pv_new)
        o_ref[...] = acc_ref[...] / l_ref[...]


def _sample_attention(page_table, rel_bias, cache_k, cache_v, cache_kidx, q, qi, kw, kb, vb, n_seq, n_new):
    n_pages = page_table.shape[1]
    n_pool = cache_k.shape[0]
    n_past = n_pages * PAGE_SIZE
    topk = min(TOPK_MAX, (n_past + n_new) // 4)
    P = min(16, n_pages)
    assert n_pages % P == 0 and n_new == SUBLANES
    n_steps = n_pages // P
    n_rows = N_HEADS * n_new
    page_cols = N_KV_HEADS * PAGE_SIZE

    qh = qi.reshape(n_seq, n_new, IDX_HEADS, IDX_DIM).transpose(0, 2, 1, 3).reshape(n_seq, n_rows, IDX_DIM)
    wc = kw[:, IDX_DIM:IDX_DIM + IDX_HEADS].reshape(n_seq, n_new, IDX_HEADS).transpose(0, 2, 1)
    wc = wc.reshape(n_seq, n_rows, 1)
    pad = ((0, 0), (0, LANES - n_new), (0, 0))
    kin = jnp.pad(kw[:, :IDX_DIM].astype(bf16).reshape(n_seq, n_new, IDX_DIM), pad)
    qs = q.reshape(n_seq, n_new, N_HEADS, HEAD_DIM).transpose(0, 2, 1, 3).reshape(n_seq, n_rows, HEAD_DIM)
    kn = jnp.pad(kb.reshape(n_seq, n_new, KV_DIM), pad)
    vn = jnp.pad(vb.reshape(n_seq, n_new, KV_DIM), pad)
    ck = cache_k.reshape(n_pool, page_cols, HEAD_DIM)
    cv = cache_v.reshape(n_pool, page_cols, HEAD_DIM)
    cki = cache_kidx.reshape(n_pool, PAGE_SIZE, IDX_DIM)

    def seq_block(shape):
        nd = len(shape)
        return pl.BlockSpec((None,) + shape, lambda b, c, pt: (b,) + (0,) * nd)

    def page_block(shape, pp):
        return pl.BlockSpec((None,) + shape, lambda b, c, pt: (pt[b, c * P + pp], 0, 0))

    past_keys, new_keys = pl.pallas_call(
        functools.partial(_sample_index_kernel, pages_per_step=P, n_new=n_new),
        grid_spec=pltpu.PrefetchScalarGridSpec(
            num_scalar_prefetch=1,
            grid=(n_seq, n_steps),
            in_specs=[seq_block((n_rows, IDX_DIM)), seq_block((n_rows, 1)), seq_block((LANES, IDX_DIM))]
                     + [page_block((PAGE_SIZE, IDX_DIM), pp) for pp in range(P)],
            out_specs=[pl.BlockSpec((None, n_new, P * PAGE_SIZE), lambda b, c, pt: (b, 0, c)),
                       seq_block((n_new, LANES))],
        ),
        out_shape=[jax.ShapeDtypeStruct((n_seq, n_new, n_past), i32),
                   jax.ShapeDtypeStruct((n_seq, n_new, LANES), i32)],
        compiler_params=pltpu.CompilerParams(
            dimension_semantics=("arbitrary", "arbitrary"), vmem_limit_bytes=VMEM_LIMIT),
        name="sample_index",
    )(page_table, qh, wc, kin, *([cki] * P))

    out = pl.pallas_call(
        functools.partial(_sample_attn_kernel, pages_per_step=P, n_new=n_new, n_past=n_past, topk=topk),
        grid_spec=pltpu.PrefetchScalarGridSpec(
            num_scalar_prefetch=1,
            grid=(n_seq, n_steps),
            in_specs=[_smem_spec(), seq_block((n_new, n_past)), seq_block((n_new, LANES)),
                      seq_block((n_rows, HEAD_DIM)), seq_block((LANES, KV_DIM)), seq_block((LANES, KV_DIM))]
                     + [page_block((page_cols, HEAD_DIM), pp) for pp in range(P)]
                     + [page_block((page_cols, HEAD_DIM), pp) for pp in range(P)],
            out_specs=seq_block((n_rows, HEAD_DIM)),
            scratch_shapes=[
                pltpu.VMEM((n_new, n_past + LANES), i32),
                pltpu.VMEM((n_rows, P * page_cols), f32),
                pltpu.VMEM((n_new, 1), i32),
                pltpu.VMEM((n_rows, 1), f32),
                pltpu.VMEM((n_rows, 1), f32),
                pltpu.VMEM((n_rows, HEAD_DIM), f32),
            ],
        ),
        out_shape=jax.ShapeDtypeStruct((n_seq, n_rows, HEAD_DIM), f32),
        compiler_params=pltpu.CompilerParams(
            dimension_semantics=("arbitrary", "arbitrary"), vmem_limit_bytes=VMEM_LIMIT),
        name="sample_attn",
    )(page_table, rel_bias, past_keys, new_keys, qs, kn, vn, *([ck] * P), *([cv] * P))

    out = out.reshape(n_seq, N_HEADS, n_new, HEAD_DIM).transpose(0, 2, 1, 3)
    return out.reshape(n_seq * n_new, N_HEADS * HEAD_DIM).astype(bf16)


def _rglru_kernel(u_ref, buf_ref, h0_ref, cw_ref, cb_ref, wg_ref, bg_ref, lam_ref,
                  y_ref, nbuf_ref, hT_ref,
                  ext_ref, a_ref, b_ref, hs_ref, h_ref, *, tt):
    t = pl.program_id(1)
    head = SUBLANES

    @pl.when(t == 0)
    def _():
        ext_ref[head - (CONV_W - 1):head, :] = buf_ref[...]
        h_ref[...] = h0_ref[...]

    ext_ref[head:head + tt, :] = u_ref[...]
    xc = cb_ref[...] + ext_ref[head - 3:head - 3 + tt, :] * cw_ref[0:1, :]
    for j in range(1, CONV_W):
        xc = xc + ext_ref[head - 3 + j:head - 3 + j + tt, :] * cw_ref[j:j + 1, :]
    tail = ext_ref[head + tt - (CONV_W - 1):head + tt, :]
    nbuf_ref[...] = tail
    ext_ref[head - (CONV_W - 1):head, :] = tail

    lam = lam_ref[...]
    neg = -lam
    softplus = jnp.maximum(neg, 0.0) + jnp.log1p(jnp.exp(-jnp.abs(neg)))
    for n in range(LRU_BLOCKS):
        ns = slice(n * LRU_BLOCK_W, (n + 1) * LRU_BLOCK_W)
        xn = xc[:, ns]
        gates = _dot(xn.astype(bf16), wg_ref[n]) + bg_ref[n]
        r = jax.nn.sigmoid(gates[:, :LRU_BLOCK_W])
        ig = jax.nn.sigmoid(gates[:, LRU_BLOCK_W:])
        log_a = (-LRU_C) * r * softplus[:, ns]
        a = jnp.exp(log_a)
        a_ref[:, ns] = a
        b_ref[:, ns] = jnp.sqrt(-jnp.tanh(log_a) * (1.0 + a * a)) * (ig * xn)

    def step(r_, h):
        h = a_ref[pl.ds(r_, 1), :] * h + b_ref[pl.ds(r_, 1), :]
        hs_ref[pl.ds(r_, 1), :] = h
        return h

    h = lax.fori_loop(0, tt, step, h_ref[...], unroll=8)
    h_ref[...] = h
    hT_ref[...] = h
    y_ref[...] = hs_ref[...].astype(y_ref.dtype)


def _rglru(u, conv_buf, h0, conv_w, conv_b, w_rg, b_rg, w_ig, b_ig, lam):
    nb, seq, _ = u.shape
    tt = min(256, seq)
    assert seq % tt == 0 and tt % SUBLANES == 0 and tt >= CONV_W - 1
    wg = jnp.concatenate([w_rg, w_ig], axis=-1).astype(bf16)
    bg = jnp.concatenate([b_rg, b_ig], axis=-1).reshape(LRU_BLOCKS, 1, 2 * LRU_BLOCK_W)

    def per_seq(rows):
        return pl.BlockSpec((None, rows, D_RNN), lambda b, t: (b, 0, 0))

    y, nbuf, hT = pl.pallas_call(
        functools.partial(_rglru_kernel, tt=tt),
        grid=(nb, seq // tt),
        in_specs=[pl.BlockSpec((None, tt, D_RNN), lambda b, t: (b, t, 0)),
                  per_seq(CONV_W - 1), per_seq(1),
                  _const_spec((CONV_W, D_RNN)), _const_spec((1, D_RNN)),
                  _const_spec(wg.shape), _const_spec(bg.shape), _const_spec((1, D_RNN))],
        out_specs=[pl.BlockSpec((None, tt, D_RNN), lambda b, t: (b, t, 0)),
                   per_seq(CONV_W - 1), per_seq(1)],
        out_shape=[jax.ShapeDtypeStruct((nb, seq, D_RNN), bf16),
                   jax.ShapeDtypeStruct((nb, CONV_W - 1, D_RNN), f32),
                   jax.ShapeDtypeStruct((nb, 1, D_RNN), f32)],
        scratch_shapes=[
            pltpu.VMEM((SUBLANES + tt, D_RNN), f32),
            pltpu.VMEM((tt, D_RNN), f32),
            pltpu.VMEM((tt, D_RNN), f32),
            pltpu.VMEM((tt, D_RNN), f32),
            pltpu.VMEM((1, D_RNN), f32),
        ],
        compiler_params=pltpu.CompilerParams(
            dimension_semantics=("arbitrary", "arbitrary"), vmem_limit_bytes=VMEM_LIMIT),
        name="rglru",
    )(u, conv_buf, h0.reshape(nb, 1, D_RNN), conv_w, conv_b.reshape(1, D_RNN), wg, bg,
      lam.reshape(1, D_RNN))
    return y, nbuf, hT.reshape(nb, D_RNN)


def _merge_ffn_kernel(x_ref, attn_ref, lru_ref, ga_ref, gb_ref, woa_ref, wol_ref, wout_ref,
                      gf_ref, wfg_ref, wfu_ref, wfd_ref, gfin_ref, y_ref):
    merged = (jax.nn.sigmoid(ga_ref[...]) * _dot(attn_ref[...], woa_ref[...])
              + jax.nn.sigmoid(gb_ref[...]) * _dot(lru_ref[...], wol_ref[...]))
    h = x_ref[...] + _dot(merged.astype(bf16), wout_ref[...])
    hn = (h * lax.rsqrt(jnp.mean(h * h, axis=-1, keepdims=True) + EPS)) * gf_ref[...]
    hn = hn.astype(bf16)
    act = jax.nn.silu(_dot(hn, wfg_ref[...])) * _dot(hn, wfu_ref[...])
    y = h + _dot(act.astype(bf16), wfd_ref[...])
    y_ref[...] = (y * lax.rsqrt(jnp.mean(y * y, axis=-1, keepdims=True) + EPS)) * gfin_ref[...]


def _merge_ffn(x2, attn, lru, ga, gb, weights, g_ffn, g_final):
    n = x2.shape[0]
    tm = min(256, n)
    assert n % tm == 0
    woa, wol, wout, wfg, wfu, wfd = weights

    def rows(width):
        return pl.BlockSpec((tm, width), lambda i: (i, 0))

    return pl.pallas_call(
        _merge_ffn_kernel,
        grid=(n // tm,),
        in_specs=[rows(D_MODEL), rows(N_HEADS * HEAD_DIM), rows(D_RNN), rows(D_MODEL), rows(D_MODEL),
                  _const_spec(woa.shape), _const_spec(wol.shape), _const_spec(wout.shape),
                  _const_spec((1, D_MODEL)), _const_spec(wfg.shape), _const_spec(wfu.shape),
                  _const_spec(wfd.shape), _const_spec((1, D_MODEL))],
        out_specs=rows(D_MODEL),
        out_shape=jax.ShapeDtypeStruct((n, D_MODEL), f32),
        compiler_params=pltpu.CompilerParams(
            dimension_semantics=("arbitrary",), vmem_limit_bytes=VMEM_LIMIT),
        name="merge_ffn",
    )(x2, attn, lru, ga, gb, woa, wol, wout, g_ffn.reshape(1, D_MODEL), wfg, wfu, wfd,
      g_final.reshape(1, D_MODEL))


def kernel(x_prompt, x_sample, cache_k, cache_v, cache_kidx, state_conv, state_rnn, page_table,
           rel_bias, g_mix, w_in, conv_w, conv_b, w_rgate, b_rgate, w_igate, b_igate, lru_lambda,
           w_o_attn, w_o_lru, w_out, g_ffn, w_ffn_gate, w_ffn_up, w_ffn_down, g_final):
    assert w_in.shape[0] == 1, "one trunk layer"
    batch, seq, _ = x_prompt.shape
    n_seq, n_new, _ = x_sample.shape
    layer = 0

    w_parts = _split_w_in(w_in[layer])
    lru_w = (conv_w[layer], conv_b[layer], w_rgate[layer], b_rgate[layer], w_igate[layer],
             b_igate[layer], lru_lambda[layer])
    out_w = tuple(w[layer].astype(bf16)
                  for w in (w_o_attn, w_o_lru, w_out, w_ffn_gate, w_ffn_up, w_ffn_down))

    xp = x_prompt.reshape(batch * seq, D_MODEL)
    q, k, v, kb, vb, qi, kie, kio, kw, u, ga, gb = _proj(xp, g_mix[layer], w_parts)
    attn = _prompt_attn(rel_bias, q, qi, kw, kb, vb, kie, kio, batch, seq)
    lru, buf_p, h_p = _rglru(u.reshape(batch, seq, D_RNN),
                             jnp.zeros((batch, CONV_W - 1, D_RNN), f32),
                             jnp.zeros((batch, D_RNN), f32), *lru_w)
    y_prompt = _merge_ffn(xp, attn, lru.reshape(batch * seq, D_RNN), ga, gb, out_w,
                          g_ffn[layer], g_final).reshape(batch, seq, D_MODEL)
    new_k_prompt = k.reshape(1, batch, seq, N_KV_HEADS, HEAD_DIM)
    new_v_prompt = v.reshape(1, batch, seq, N_KV_HEADS, HEAD_DIM)
    new_kidx_prompt = kw[:, :IDX_DIM].reshape(1, batch, seq, IDX_DIM)

    xs = x_sample.reshape(n_seq * n_new, D_MODEL)
    q, k, v, kb, vb, qi, kie, kio, kw, u, ga, gb = _proj(xs, g_mix[layer], w_parts)
    attn = _sample_attention(page_table, rel_bias, cache_k[layer], cache_v[layer], cache_kidx[layer],
                             q, qi, kw, kb, vb, n_seq, n_new)
    lru, buf_s, h_s = _rglru(u.reshape(n_seq, n_new, D_RNN), state_conv[layer], state_rnn[layer], *lru_w)
    y_sample = _merge_ffn(xs, attn, lru.reshape(n_seq * n_new, D_RNN), ga, gb, out_w,
                          g_ffn[layer], g_final).reshape(n_seq, n_new, D_MODEL)
    new_k_sample = k.reshape(1, n_seq, n_new, N_KV_HEADS, HEAD_DIM)
    new_v_sample = v.reshape(1, n_seq, n_new, N_KV_HEADS, HEAD_DIM)
    new_kidx_sample = kw[:, :IDX_DIM].reshape(1, n_seq, n_new, IDX_DIM)

    return (y_prompt, y_sample, new_k_prompt, new_v_prompt, new_kidx_prompt, buf_p[None], h_p[None],
            new_k_sample, new_v_sample, new_kidx_sample, buf_s[None], h_s[None])
```

```python
import functools
import math

import jax
import jax.numpy as jnp
from jax import lax
from jax.experimental import pallas as pl
from jax.experimental.pallas import tpu as pltpu

D_MODEL = 1024
N_HEADS = 8
N_KV_HEADS = 2
HEAD_DIM = 128
GROUP = N_HEADS // N_KV_HEADS
IDX_HEADS = 8
IDX_DIM = 64
TOPK_MAX = 256
D_RNN = D_MODEL
LRU_BLOCKS = 8
LRU_BLOCK_W = D_RNN // LRU_BLOCKS
CONV_W = 4
LRU_C = 8.0
N_BUCKETS = 32
MAX_EXACT = N_BUCKETS // 2
MAX_DISTANCE = 128
EPS = 1e-6
NEG_INF = -1e30
PAGE_SIZE = 128
KV_DIM = N_KV_HEADS * HEAD_DIM
LOG2E = math.log2(math.e)
Q_SCALE = HEAD_DIM ** -0.5 * LOG2E
WI_SCALE = IDX_HEADS ** -0.5 * IDX_DIM ** -0.5

LANES = 128
SUBLANES = 8
BF16_SUBLANES = 16
VMEM_LIMIT = 56 * 1024 * 1024

_BUCKET_STEPS = tuple(
    math.ceil(MAX_EXACT * (MAX_DISTANCE / MAX_EXACT) ** (k / (N_BUCKETS - MAX_EXACT)))
    for k in range(1, N_BUCKETS - MAX_EXACT))
M_INIT = -1e29

_INT_MIN = -2 ** 31
_F32_MANTISSA_MASK = 0x007FFFFF
_INF_KEY = 0x7F800000 - _F32_MANTISSA_MASK
bf16 = jnp.bfloat16
f32 = jnp.float32
i32 = jnp.int32


def _dot(a, b):
    return jnp.dot(a, b, preferred_element_type=f32)


def _dot_nt(a, b):
    return lax.dot_general(a, b, (((1,), (1,)), ((), ())), preferred_element_type=f32)


def _const_spec(shape):
    zeros = (0,) * len(shape)
    return pl.BlockSpec(shape, lambda *_: zeros, pipeline_mode=pl.Buffered(1))


def _smem_spec():
    return pl.BlockSpec(memory_space=pltpu.SMEM)


def _key_to_float(key):
    mag = jnp.minimum(jnp.abs(jnp.maximum(key, -_INF_KEY)), _INF_KEY)
    bits = jnp.where(mag == 0, 0, mag + _F32_MANTISSA_MASK)
    return lax.bitcast_convert_type(jnp.where(key < 0, bits | _INT_MIN, bits), f32)


def _bucket(dist):
    d = jnp.maximum(dist, 0)
    large = jnp.full(d.shape, MAX_EXACT, i32)
    for step in _BUCKET_STEPS:
        large = large + (d >= step).astype(i32)
    return jnp.where(d < MAX_EXACT, d, large)


def _proj_kernel(x_ref, g_ref, wq_ref, wkv_ref, wqi_ref, wke_ref, wko_ref, wkw_ref, wu_ref,
                 wga_ref, wgb_ref,
                 q_ref, k_ref, v_ref, kb_ref, vb_ref, qi_ref, kie_ref, kio_ref, kw_ref,
                 u_ref, ga_ref, gb_ref):
    x = x_ref[...]
    ms = jnp.mean(x * x, axis=-1, keepdims=True)
    xn = ((x * lax.rsqrt(ms + EPS)) * g_ref[...]).astype(bf16)

    q_ref[...] = (_dot(xn, wq_ref[...]) * Q_SCALE).astype(bf16)
    kv = _dot(xn, wkv_ref[...])
    k_ref[...] = kv[:, :KV_DIM]
    v_ref[...] = kv[:, KV_DIM:]
    kb_ref[...] = kv[:, :KV_DIM].astype(bf16)
    vb_ref[...] = kv[:, KV_DIM:].astype(bf16)
    qi_ref[...] = _dot(xn, wqi_ref[...]).astype(bf16)
    kie_ref[...] = _dot(xn, wke_ref[...]).astype(bf16)
    kio_ref[...] = _dot(xn, wko_ref[...]).astype(bf16)
    kw = _dot(xn, wkw_ref[...])
    lane = lax.broadcasted_iota(i32, kw.shape, 1)
    kw_ref[...] = jnp.where(lane >= IDX_DIM, kw * WI_SCALE, kw)
    u_ref[...] = _dot(xn, wu_ref[...])
    ga_ref[...] = _dot(xn, wga_ref[...])
    gb_ref[...] = _dot(xn, wgb_ref[...])


def _split_w_in(w_in):
    sizes = (N_HEADS * HEAD_DIM, KV_DIM, KV_DIM, IDX_HEADS * IDX_DIM, IDX_DIM, IDX_HEADS,
             D_RNN, D_MODEL, D_MODEL)
    parts, acc = [], 0
    for s in sizes:
        parts.append(w_in[:, acc:acc + s])
        acc += s
    wq, wk, wv, wqi, wki, wwi, wu, wga, wgb = parts
    zk = jnp.zeros_like(wki)
    wkv = jnp.concatenate([wk, wv], axis=1)
    wke = jnp.concatenate([wki, zk], axis=1)
    wko = jnp.concatenate([zk, wki], axis=1)
    wkw = jnp.concatenate(
        [wki, wwi, jnp.zeros((w_in.shape[0], LANES - IDX_DIM - IDX_HEADS), w_in.dtype)], axis=1)
    return tuple(w.astype(bf16) for w in (wq, wkv, wqi, wke, wko, wkw, wu, wga, wgb))


def _proj(x2, g_mix, w_parts):
    n = x2.shape[0]
    tm = min(512, n)
    assert n % tm == 0

    def rows(width):
        return pl.BlockSpec((tm, width), lambda i: (i, 0))

    out_widths = (N_HEADS * HEAD_DIM, KV_DIM, KV_DIM, KV_DIM, KV_DIM, IDX_HEADS * IDX_DIM,
                  LANES, LANES, LANES, D_RNN, D_MODEL, D_MODEL)
    out_dtypes = (bf16, f32, f32, bf16, bf16, bf16, bf16, bf16, f32, f32, f32, f32)
    return pl.pallas_call(
        _proj_kernel,
        grid=(n // tm,),
        in_specs=[rows(D_MODEL), _const_spec((1, D_MODEL))] + [_const_spec(w.shape) for w in w_parts],
        out_specs=[rows(w) for w in out_widths],
        out_shape=[jax.ShapeDtypeStruct((n, w), d) for w, d in zip(out_widths, out_dtypes)],
        compiler_params=pltpu.CompilerParams(
            dimension_semantics=("arbitrary",), vmem_limit_bytes=VMEM_LIMIT),
        name="proj",
    )(x2, g_mix.reshape(1, D_MODEL), *w_parts)


def _kth_largest(count_ge, n_all, k):
    def body(it, carry):
        t_u, n_at = carry
        cand_u = t_u | lax.shift_left(jnp.int32(1), jnp.int32(31) - it)
        n = count_ge(_key_to_float(cand_u ^ _INT_MIN))
        ok = n >= k
        return jnp.where(ok, cand_u, t_u), jnp.where(ok, n, n_at)

    t_u, n_at = lax.fori_loop(0, 32, body, (jnp.zeros(n_all.shape, i32), n_all))
    return t_u ^ _INT_MIN, n_at


def _tie_cut(count_eq_below, shape, need, n_bits):
    def body(it, x):
        bit = lax.shift_left(jnp.int32(1), jnp.int32(n_bits - 1) - it)
        cand = x | bit
        cnt = count_eq_below(cand)
        return jnp.where(cnt < need, cand, x)

    return lax.fori_loop(0, n_bits, body, jnp.zeros(shape, i32))


def _prompt_attn_kernel(rb_ref, q_ref, qi_ref, kw_ref, kb_ref, vb_ref, kie_ref, kio_ref, o_ref,
                        score_ref, mask_ref, bias_ref, mx_ref, acc_ref, *, tq, topk, seq):
    b = pl.program_id(0)
    i = pl.program_id(1)
    tk = tq
    krow = lax.broadcasted_iota(i32, (tk, tq), 0)
    qcol = lax.broadcasted_iota(i32, (tk, tq), 1)

    @pl.when((b == 0) & (i == 0))
    def _():
        for t in range(2):
            bucket = _bucket(t * tq + krow - qcol)
            for h in range(N_HEADS):
                bias_ref[t * N_HEADS + h] = jnp.zeros((tq, tk), f32)

            def fill(bb, carry, t=t, bucket=bucket):
                hit = bucket == bb
                for h in range(N_HEADS):
                    bias_ref[t * N_HEADS + h] = jnp.where(
                        hit, rb_ref[bb, h] * LOG2E, bias_ref[t * N_HEADS + h])
                return carry

            lax.fori_loop(0, N_BUCKETS, fill, 0)

    w_t = kw_ref[...].T

    def score_block(j, causal):
        off = pl.multiple_of(j * tk, tk)
        ke = kie_ref[pl.ds(off, tk), :]
        ko = kio_ref[pl.ds(off, tk), :]
        s = jnp.zeros((tk, tq), f32)
        for p in range(IDX_HEADS // 2):
            rhs = qi_ref[:, p * LANES:(p + 1) * LANES]
            c0 = IDX_DIM + 2 * p
            s = s + jnp.maximum(_dot_nt(ke, rhs), 0.0) * w_t[c0:c0 + 1, :]
            s = s + jnp.maximum(_dot_nt(ko, rhs), 0.0) * w_t[c0 + 1:c0 + 2, :]
        if causal:
            s = jnp.where(krow <= qcol, s, NEG_INF)
        score_ref[pl.ds(off, tk), :] = s

    def score_loop(j, carry):
        score_block(j, False)
        return carry

    lax.fori_loop(0, i, score_loop, 0)
    score_block(i, True)

    n_part = 4

    def score_count(pred):
        def body(j, c):
            off = pl.multiple_of(j * tk, tk)
            hit = pred(score_ref[pl.ds(off, tk), :], off).astype(i32)
            parts = [hit[g * SUBLANES:(g + 1) * SUBLANES] for g in range(n_part)]
            for g in range(n_part, tk // SUBLANES):
                parts[g % n_part] = parts[g % n_part] + hit[g * SUBLANES:(g + 1) * SUBLANES]
            return c + ((parts[0] + parts[1]) + (parts[2] + parts[3]))

        c = lax.fori_loop(0, i + 1, body, jnp.zeros((SUBLANES, tq), i32))
        return jnp.sum(c, axis=0, keepdims=True)

    def count_ge(t):
        return score_count(lambda sc, off: sc >= t)

    thr_key, n_ge = _kth_largest(count_ge, jnp.full((1, tq), (i + 1) * tk, i32), topk)
    thr = _key_to_float(thr_key)

    @pl.when(jnp.max(n_ge) > topk)
    def _():
        need = topk - count_ge(_key_to_float(thr_key + 1))
        cut = _tie_cut(lambda x: score_count(lambda sc, off: (sc == thr) & (krow + off < x)),
                       (1, tq), need, (seq - 1).bit_length())
        below = _key_to_float(thr_key - 1)

        def demote(j, carry):
            off = pl.multiple_of(j * tk, tk)
            sc = score_ref[pl.ds(off, tk), :]
            score_ref[pl.ds(off, tk), :] = jnp.where((sc == thr) & (krow + off > cut), below, sc)
            return carry

        lax.fori_loop(0, i + 1, demote, 0)

    far_bias = [rb_ref[N_BUCKETS - 1, h] * LOG2E for h in range(N_HEADS)]

    def to_mask(j, causal):
        off = pl.multiple_of(j * tk, tk)
        sel = score_ref[pl.ds(off, tk), :] >= thr
        if causal:
            sel = sel & (krow <= qcol)
        mask_ref[:, pl.ds(off, tk)] = jnp.where(sel, 0.0, NEG_INF).T

    def mask_loop(j, carry):
        to_mask(j, False)
        return carry

    lax.fori_loop(0, i, mask_loop, 0)
    to_mask(i, True)

    def group_logits(j, mode):
        off = pl.multiple_of(j * tk, tk)
        mask = mask_ref[:, pl.ds(off, tk)]
        for n in range(N_KV_HEADS):
            ns = slice(n * HEAD_DIM, (n + 1) * HEAD_DIM)
            heads = range(n * GROUP, (n + 1) * GROUP)
            qs = jnp.concatenate([q_ref[:, h * HEAD_DIM:(h + 1) * HEAD_DIM] for h in heads], axis=0)
            s = _dot_nt(qs, kb_ref[pl.ds(off, tk), ns])
            per_head = []
            for g, h in enumerate(heads):
                sg = s[g * tq:(g + 1) * tq] + mask
                if mode != 0:
                    sg = sg + bias_ref[(2 - mode) * N_HEADS + h]
                per_head.append((h, sg))
            yield n, off, ns, per_head

    mx_ref[...] = jnp.full(mx_ref.shape, NEG_INF, f32)

    def max_block(j, mode):
        for _, _, _, per_head in group_logits(j, mode):
            for h, sg in per_head:
                part = sg[:, 0:LANES]
                for c in range(1, tk // LANES):
                    part = jnp.maximum(part, sg[:, c * LANES:(c + 1) * LANES])
                if mode == 0:
                    part = part + far_bias[h]
                mx_ref[h] = jnp.maximum(mx_ref[h], part)

    def far_max(j, carry):
        max_block(j, 0)
        return carry

    lax.fori_loop(0, i - 1, far_max, 0)

    @pl.when(i >= 1)
    def _():
        max_block(i - 1, 1)

    max_block(i, 2)
    for h in range(N_HEADS):
        mx_ref[h] = jnp.broadcast_to(jnp.max(mx_ref[h], axis=-1, keepdims=True), (tq, LANES))

    acc_ref[...] = jnp.zeros(acc_ref.shape, f32)
    ones_cols = jnp.ones((tk, HEAD_DIM), bf16)

    def pv_block(j, mode):
        for n, off, ns, per_head in group_logits(j, mode):
            ps = []
            for h, sg in per_head:
                shift = mx_ref[h] - far_bias[h] if mode == 0 else mx_ref[h]
                shift = jnp.concatenate([shift] * (tk // LANES), axis=1)
                ps.append(jnp.exp2(sg - shift).astype(bf16))
            values = jnp.concatenate([vb_ref[pl.ds(off, tk), ns], ones_cols], axis=1)
            acc_ref[n] = acc_ref[n] + _dot(jnp.concatenate(ps, axis=0), values)

    def far_pv(j, carry):
        pv_block(j, 0)
        return carry

    lax.fori_loop(0, i - 1, far_pv, 0)

    @pl.when(i >= 1)
    def _():
        pv_block(i - 1, 1)

    pv_block(i, 2)

    for h in range(N_HEADS):
        acc = acc_ref[h // GROUP, (h % GROUP) * tq:(h % GROUP + 1) * tq, :]
        out = acc[:, :HEAD_DIM] / acc[:, HEAD_DIM:]
        o_ref[:, h * HEAD_DIM:(h + 1) * HEAD_DIM] = out.astype(o_ref.dtype)


def _prompt_attn(rel_bias, q, qi, kw, kb, vb, kie, kio, batch, seq):
    tq = min(256, seq)
    assert seq % tq == 0 and tq % LANES == 0 and tq >= MAX_DISTANCE
    nq = seq // tq
    topk = min(TOPK_MAX, seq // 4)

    def qrows(width):
        return pl.BlockSpec((tq, width), lambda b, i: (b * nq + i, 0))

    def krows(width):
        return pl.BlockSpec((seq, width), lambda b, i: (b, 0))

    kern = functools.partial(_prompt_attn_kernel, tq=tq, topk=topk, seq=seq)
    return pl.pallas_call(
        kern,
        grid=(batch, nq),
        in_specs=[_smem_spec(), qrows(N_HEADS * HEAD_DIM), qrows(IDX_HEADS * IDX_DIM), qrows(LANES),
                  krows(KV_DIM), krows(KV_DIM), krows(LANES), krows(LANES)],
        out_specs=qrows(N_HEADS * HEAD_DIM),
        out_shape=jax.ShapeDtypeStruct((batch * seq, N_HEADS * HEAD_DIM), bf16),
        scratch_shapes=[
            pltpu.VMEM((seq, tq), f32),
            pltpu.VMEM((tq, seq), f32),
            pltpu.VMEM((2 * N_HEADS, tq, tq), f32),
            pltpu.VMEM((N_HEADS, tq, LANES), f32),
            pltpu.VMEM((N_KV_HEADS, GROUP * tq, 2 * HEAD_DIM), f32),
        ],
        compiler_params=pltpu.CompilerParams(
            dimension_semantics=("arbitrary", "arbitrary"), vmem_limit_bytes=VMEM_LIMIT),
        name="prompt_attn",
    )(rel_bias, q, qi, kw, kb, vb, kie, kio)


def _sample_index_kernel(pt_ref, qh_ref, wc_ref, kin_ref, *rest, pages_per_step, n_new):
    del pt_ref
    page_refs = rest[:pages_per_step]
    past_ref, new_ref = rest[pages_per_step:]
    c = pl.program_id(1)
    qh = qh_ref[...]
    wc = wc_ref[...]

    def scores(keys_t):
        s = jnp.maximum(_dot(qh, keys_t), 0.0) * wc
        out = s[0:n_new]
        for h in range(1, IDX_HEADS):
            out = out + s[h * n_new:(h + 1) * n_new]
        return out

    pages = jnp.concatenate([r[...].astype(bf16) for r in page_refs], axis=1)
    past_ref[...] = scores(pages)

    @pl.when(c == 0)
    def _():
        sc = scores(kin_ref[...])
        tok = lax.broadcasted_iota(i32, sc.shape, 0)
        key = lax.broadcasted_iota(i32, sc.shape, 1)
        new_ref[...] = jnp.where(key <= tok, sc, NEG_INF)


def _sample_attn_kernel(pt_ref, rb_ref, past_ref, new_ref, q_ref, kn_ref, vn_ref, *rest,
                        pages_per_step, n_new, n_past, topk):
    del pt_ref
    P = pages_per_step
    k_refs = rest[:P]
    v_refs = rest[P:2 * P]
    o_ref = rest[2 * P]
    score_ref, kcat_ref, vcat_ref, thr_ref, m_ref, l_ref, acc_ref = rest[2 * P + 1:]
    c = pl.program_id(1)
    n_steps = pl.num_programs(1)
    n_rows = N_HEADS * n_new
    half = GROUP * n_new
    step_keys = P * PAGE_SIZE
    n_keys = n_past + LANES

    def head_column(bucket_row):
        r = lax.broadcasted_iota(i32, (n_rows, 1), 0) // n_new
        out = jnp.zeros((n_rows, 1), f32)
        for h in range(N_HEADS):
            out = jnp.where(r == h, rb_ref[bucket_row, h] * LOG2E, out)
        return out

    def bias_tile(dist):
        bucket = _bucket(dist)

        def fill(bb, acc):
            return jnp.where(bucket == bb, head_column(bb), acc)

        return lax.fori_loop(0, N_BUCKETS, fill, jnp.zeros(dist.shape, f32))

    @pl.when(c == 0)
    def _():
        score_ref[:, :n_past] = past_ref[...]
        score_ref[:, n_past:] = new_ref[...]

        def count(pred):
            return jnp.sum(pred(score_ref[...]).astype(i32), axis=-1, keepdims=True)

        def count_ge(t):
            return count(lambda sc: sc >= t)

        thr_key, n_ge = _kth_largest(count_ge, jnp.full((n_new, 1), n_keys, i32), topk)
        thr = _key_to_float(thr_key)
        thr_ref[...] = thr

        @pl.when(jnp.max(n_ge) > topk)
        def _():
            need = topk - count_ge(_key_to_float(thr_key + 1))
            colk = lax.broadcasted_iota(i32, (n_new, n_keys), 1)
            cut = _tie_cut(lambda x: count(lambda sc: (sc == thr) & (colk < x)),
                           (n_new, 1), need, (n_keys - 1).bit_length())
            sc = score_ref[...]
            score_ref[...] = jnp.where((sc == thr) & (colk > cut), _key_to_float(thr_key - 1), sc)

        m_ref[...] = jnp.full(m_ref.shape, M_INIT, f32)
        l_ref[...] = jnp.zeros(l_ref.shape, f32)
        acc_ref[...] = jnp.zeros(acc_ref.shape, f32)

    thr = thr_ref[...]
    far_bias = head_column(N_BUCKETS - 1)
    tok = lax.broadcasted_iota(i32, (n_rows, LANES), 0) % n_new
    kpos = lax.broadcasted_iota(i32, (n_rows, LANES), 1)

    def online_update(n, s, values):
        rows = slice(n * half, (n + 1) * half)
        m_old = m_ref[rows]
        m_new = jnp.maximum(m_old, jnp.max(s, axis=-1, keepdims=True))
        alpha = jnp.exp2(m_old - m_new)
        p = jnp.exp2(s - m_new)
        l_ref[rows] = alpha * l_ref[rows] + jnp.sum(p, axis=-1, keepdims=True)
        acc_ref[rows] = alpha * acc_ref[rows] + _dot(p.astype(bf16), values)
        m_ref[rows] = m_new

    def tiled_mask(scores):
        mask = jnp.where(scores >= thr, 0.0, NEG_INF)
        return jnp.concatenate([mask] * GROUP, axis=0)

    for pp in range(P):
        for n in range(N_KV_HEADS):
            rows = pl.ds(n, PAGE_SIZE, stride=N_KV_HEADS)
            kcat_ref[n, pp * PAGE_SIZE:(pp + 1) * PAGE_SIZE, :] = k_refs[pp][rows, :].astype(bf16)
            vcat_ref[n, pp * PAGE_SIZE:(pp + 1) * PAGE_SIZE, :] = v_refs[pp][rows, :].astype(bf16)

    base = pl.multiple_of(c * step_keys, step_keys)
    mask = tiled_mask(score_ref[:, pl.ds(base, step_keys)])
    last_bias = lax.cond(c == n_steps - 1,
                         lambda: bias_tile(PAGE_SIZE + tok - kpos),
                         lambda: jnp.broadcast_to(far_bias, (n_rows, LANES)))
    for n in range(N_KV_HEADS):
        rows = slice(n * half, (n + 1) * half)
        s = _dot_nt(q_ref[rows, :], kcat_ref[n])
        s = jnp.concatenate([s[:, :step_keys - PAGE_SIZE] + far_bias[rows],
                             s[:, step_keys - PAGE_SIZE:] + last_bias[rows]], axis=1)
        online_update(n, s + mask, vcat_ref[n])

    @pl.when(c == n_steps - 1)
    def _():
        causal = (lax.broadcasted_iota(i32, (half, LANES), 1)
                  <= lax.broadcasted_iota(i32, (half, LANES), 0) % n_new)
        mask = jnp.where(causal, tiled_mask(score_ref[:, n_past:]), NEG_INF)
        bias = bias_tile(tok - kpos)
        for n in range(N_KV_HEADS):
            rows = slice(n * half, (n + 1) * half)
            ns = slice(n * HEAD_DIM, (n + 1) * HEAD_DIM)
            s = _dot_nt(q_ref[rows, :], kn_ref[:, ns]) + bias[rows]
            online_update(n, s + mask, vn_ref[:, ns])
        o_ref[...] = acc_ref[...] / l_ref[...]


def _sample_attention(page_table, rel_bias, cache_k, cache_v, cache_kidx, q, qi, kw, kb, vb, n_seq, n_new):
    n_pages = page_table.shape[1]
    n_pool = cache_k.shape[0]
    n_past = n_pages * PAGE_SIZE
    topk = min(TOPK_MAX, (n_past + n_new) // 4)
    P = min(32, n_pages)
    assert n_pages % P == 0 and n_new == SUBLANES
    n_steps = n_pages // P
    n_rows = N_HEADS * n_new
    page_cols = N_KV_HEADS * PAGE_SIZE

    qh = qi.reshape(n_seq, n_new, IDX_HEADS, IDX_DIM).transpose(0, 2, 1, 3).reshape(n_seq, n_rows, IDX_DIM)
    wc = kw[:, IDX_DIM:IDX_DIM + IDX_HEADS].reshape(n_seq, n_new, IDX_HEADS).transpose(0, 2, 1)
    wc = wc.reshape(n_seq, n_rows, 1)
    kin = kw[:, :IDX_DIM].astype(bf16).reshape(n_seq, n_new, IDX_DIM).transpose(0, 2, 1)
    kin = jnp.pad(kin, ((0, 0), (0, 0), (0, LANES - n_new)))
    qs = q.reshape(n_seq, n_new, N_HEADS, HEAD_DIM).transpose(0, 2, 1, 3).reshape(n_seq, n_rows, HEAD_DIM)
    pad = ((0, 0), (0, LANES - n_new), (0, 0))
    kn = jnp.pad(kb.reshape(n_seq, n_new, KV_DIM), pad)
    vn = jnp.pad(vb.reshape(n_seq, n_new, KV_DIM), pad)
    ck = cache_k.reshape(n_pool, page_cols, HEAD_DIM)
    cv = cache_v.reshape(n_pool, page_cols, HEAD_DIM)
    cki = jnp.swapaxes(cache_kidx, -1, -2)

    def seq_block(shape):
        nd = len(shape)
        return pl.BlockSpec((None,) + shape, lambda b, c, pt: (b,) + (0,) * nd)

    def page_block(shape, pp):
        return pl.BlockSpec((None,) + shape, lambda b, c, pt: (pt[b, c * P + pp], 0, 0))

    past_keys, new_keys = pl.pallas_call(
        functools.partial(_sample_index_kernel, pages_per_step=P, n_new=n_new),
        grid_spec=pltpu.PrefetchScalarGridSpec(
            num_scalar_prefetch=1,
            grid=(n_seq, n_steps),
            in_specs=[seq_block((n_rows, IDX_DIM)), seq_block((n_rows, 1)), seq_block((IDX_DIM, LANES))]
                     + [page_block((IDX_DIM, PAGE_SIZE), pp) for pp in range(P)],
            out_specs=[pl.BlockSpec((None, n_new, P * PAGE_SIZE), lambda b, c, pt: (b, 0, c)),
                       seq_block((n_new, LANES))],
        ),
        out_shape=[jax.ShapeDtypeStruct((n_seq, n_new, n_past), f32),
                   jax.ShapeDtypeStruct((n_seq, n_new, LANES), f32)],
        compiler_params=pltpu.CompilerParams(
            dimension_semantics=("arbitrary", "arbitrary"), vmem_limit_bytes=VMEM_LIMIT),
        name="sample_index",
    )(page_table, qh, wc, kin, *([cki] * P))

    out = pl.pallas_call(
        functools.partial(_sample_attn_kernel, pages_per_step=P, n_new=n_new, n_past=n_past, topk=topk),
        grid_spec=pltpu.PrefetchScalarGridSpec(
            num_scalar_prefetch=1,
            grid=(n_seq, n_steps),
            in_specs=[_smem_spec(), seq_block((n_new, n_past)), seq_block((n_new, LANES)),
                      seq_block((n_rows, HEAD_DIM)), seq_block((LANES, KV_DIM)), seq_block((LANES, KV_DIM))]
                     + [page_block((page_cols, HEAD_DIM), pp) for pp in range(P)]
                     + [page_block((page_cols, HEAD_DIM), pp) for pp in range(P)],
            out_specs=seq_block((n_rows, HEAD_DIM)),
            scratch_shapes=[
                pltpu.VMEM((n_new, n_past + LANES), f32),
                pltpu.VMEM((N_KV_HEADS, P * PAGE_SIZE, HEAD_DIM), bf16),
                pltpu.VMEM((N_KV_HEADS, P * PAGE_SIZE, HEAD_DIM), bf16),
                pltpu.VMEM((n_new, 1), f32),
                pltpu.VMEM((n_rows, 1), f32),
                pltpu.VMEM((n_rows, 1), f32),
                pltpu.VMEM((n_rows, HEAD_DIM), f32),
            ],
        ),
        out_shape=jax.ShapeDtypeStruct((n_seq, n_rows, HEAD_DIM), f32),
        compiler_params=pltpu.CompilerParams(
            dimension_semantics=("arbitrary", "arbitrary"), vmem_limit_bytes=VMEM_LIMIT),
        name="sample_attn",
    )(page_table, rel_bias, past_keys, new_keys, qs, kn, vn, *([ck] * P), *([cv] * P))

    out = out.reshape(n_seq, N_HEADS, n_new, HEAD_DIM).transpose(0, 2, 1, 3)
    return out.reshape(n_seq * n_new, N_HEADS * HEAD_DIM).astype(bf16)


def _rglru_kernel(u_ref, buf_ref, h0_ref, cw_ref, cb_ref, wg_ref, bg_ref, lam_ref,
                  y_ref, nbuf_ref, hT_ref,
                  ext_ref, a_ref, b_ref, hs_ref, h_ref, *, tt):
    t = pl.program_id(1)
    head = SUBLANES

    @pl.when(t == 0)
    def _():
        ext_ref[head - (CONV_W - 1):head, :] = buf_ref[...]
        h_ref[...] = h0_ref[...]

    ext_ref[head:head + tt, :] = u_ref[...]
    xc = cb_ref[...] + ext_ref[head - 3:head - 3 + tt, :] * cw_ref[0:1, :]
    for j in range(1, CONV_W):
        xc = xc + ext_ref[head - 3 + j:head - 3 + j + tt, :] * cw_ref[j:j + 1, :]
    tail = ext_ref[head + tt - (CONV_W - 1):head + tt, :]
    nbuf_ref[...] = tail
    ext_ref[head - (CONV_W - 1):head, :] = tail

    lam = lam_ref[...]
    neg = -lam
    softplus = jnp.maximum(neg, 0.0) + jnp.log1p(jnp.exp(-jnp.abs(neg)))
    for n in range(LRU_BLOCKS):
        ns = slice(n * LRU_BLOCK_W, (n + 1) * LRU_BLOCK_W)
        xn = xc[:, ns]
        gates = _dot(xn.astype(bf16), wg_ref[n]) + bg_ref[n]
        r = jax.nn.sigmoid(gates[:, :LRU_BLOCK_W])
        ig = jax.nn.sigmoid(gates[:, LRU_BLOCK_W:])
        log_a = (-LRU_C) * r * softplus[:, ns]
        a = jnp.exp(log_a)
        a_ref[:, ns] = a
        b_ref[:, ns] = jnp.sqrt(-jnp.tanh(log_a) * (1.0 + a * a)) * (ig * xn)

    def step(r_, h):
        h = a_ref[pl.ds(r_, 1), :] * h + b_ref[pl.ds(r_, 1), :]
        hs_ref[pl.ds(r_, 1), :] = h
        return h

    h = lax.fori_loop(0, tt, step, h_ref[...], unroll=8)
    h_ref[...] = h
    hT_ref[...] = h
    y_ref[...] = hs_ref[...].astype(y_ref.dtype)


def _rglru(u, conv_buf, h0, conv_w, conv_b, w_rg, b_rg, w_ig, b_ig, lam):
    nb, seq, _ = u.shape
    tt = min(256, seq)
    assert seq % tt == 0 and tt % SUBLANES == 0 and tt >= CONV_W - 1
    wg = jnp.concatenate([w_rg, w_ig], axis=-1).astype(bf16)
    bg = jnp.concatenate([b_rg, b_ig], axis=-1).reshape(LRU_BLOCKS, 1, 2 * LRU_BLOCK_W)

    def per_seq(rows):
        return pl.BlockSpec((None, rows, D_RNN), lambda b, t: (b, 0, 0))

    y, nbuf, hT = pl.pallas_call(
        functools.partial(_rglru_kernel, tt=tt),
        grid=(nb, seq // tt),
        in_specs=[pl.BlockSpec((None, tt, D_RNN), lambda b, t: (b, t, 0)),
                  per_seq(CONV_W - 1), per_seq(1),
                  _const_spec((CONV_W, D_RNN)), _const_spec((1, D_RNN)),
                  _const_spec(wg.shape), _const_spec(bg.shape), _const_spec((1, D_RNN))],
        out_specs=[pl.BlockSpec((None, tt, D_RNN), lambda b, t: (b, t, 0)),
                   per_seq(CONV_W - 1), per_seq(1)],
        out_shape=[jax.ShapeDtypeStruct((nb, seq, D_RNN), bf16),
                   jax.ShapeDtypeStruct((nb, CONV_W - 1, D_RNN), f32),
                   jax.ShapeDtypeStruct((nb, 1, D_RNN), f32)],
        scratch_shapes=[
            pltpu.VMEM((SUBLANES + tt, D_RNN), f32),
            pltpu.VMEM((tt, D_RNN), f32),
            pltpu.VMEM((tt, D_RNN), f32),
            pltpu.VMEM((tt, D_RNN), f32),
            pltpu.VMEM((1, D_RNN), f32),
        ],
        compiler_params=pltpu.CompilerParams(
            dimension_semantics=("arbitrary", "arbitrary"), vmem_limit_bytes=VMEM_LIMIT),
        name="rglru",
    )(u, conv_buf, h0.reshape(nb, 1, D_RNN), conv_w, conv_b.reshape(1, D_RNN), wg, bg,
      lam.reshape(1, D_RNN))
    return y, nbuf, hT.reshape(nb, D_RNN)


def _merge_ffn_kernel(x_ref, attn_ref, lru_ref, ga_ref, gb_ref, woa_ref, wol_ref, wout_ref,
                      gf_ref, wfg_ref, wfu_ref, wfd_ref, gfin_ref, y_ref):
    merged = (jax.nn.sigmoid(ga_ref[...]) * _dot(attn_ref[...], woa_ref[...])
              + jax.nn.sigmoid(gb_ref[...]) * _dot(lru_ref[...], wol_ref[...]))
    h = x_ref[...] + _dot(merged.astype(bf16), wout_ref[...])
    hn = (h * lax.rsqrt(jnp.mean(h * h, axis=-1, keepdims=True) + EPS)) * gf_ref[...]
    hn = hn.astype(bf16)
    act = jax.nn.silu(_dot(hn, wfg_ref[...])) * _dot(hn, wfu_ref[...])
    y = h + _dot(act.astype(bf16), wfd_ref[...])
    y_ref[...] = (y * lax.rsqrt(jnp.mean(y * y, axis=-1, keepdims=True) + EPS)) * gfin_ref[...]


def _merge_ffn(x2, attn, lru, ga, gb, weights, g_ffn, g_final):
    n = x2.shape[0]
    tm = min(256, n)
    assert n % tm == 0
    woa, wol, wout, wfg, wfu, wfd = weights

    def rows(width):
        return pl.BlockSpec((tm, width), lambda i: (i, 0))

    return pl.pallas_call(
        _merge_ffn_kernel,
        grid=(n // tm,),
        in_specs=[rows(D_MODEL), rows(N_HEADS * HEAD_DIM), rows(D_RNN), rows(D_MODEL), rows(D_MODEL),
                  _const_spec(woa.shape), _const_spec(wol.shape), _const_spec(wout.shape),
                  _const_spec((1, D_MODEL)), _const_spec(wfg.shape), _const_spec(wfu.shape),
                  _const_spec(wfd.shape), _const_spec((1, D_MODEL))],
        out_specs=rows(D_MODEL),
        out_shape=jax.ShapeDtypeStruct((n, D_MODEL), f32),
        compiler_params=pltpu.CompilerParams(
            dimension_semantics=("arbitrary",), vmem_limit_bytes=VMEM_LIMIT),
        name="merge_ffn",
    )(x2, attn, lru, ga, gb, woa, wol, wout, g_ffn.reshape(1, D_MODEL), wfg, wfu, wfd,
      g_final.reshape(1, D_MODEL))


def kernel(x_prompt, x_sample, cache_k, cache_v, cache_kidx, state_conv, state_rnn, page_table,
           rel_bias, g_mix, w_in, conv_w, conv_b, w_rgate, b_rgate, w_igate, b_igate, lru_lambda,
           w_o_attn, w_o_lru, w_out, g_ffn, w_ffn_gate, w_ffn_up, w_ffn_down, g_final):
    assert w_in.shape[0] == 1, "one trunk layer"
    batch, seq, _ = x_prompt.shape
    n_seq, n_new, _ = x_sample.shape
    layer = 0

    w_parts = _split_w_in(w_in[layer])
    lru_w = (conv_w[layer], conv_b[layer], w_rgate[layer], b_rgate[layer], w_igate[layer],
             b_igate[layer], lru_lambda[layer])
    out_w = tuple(w[layer].astype(bf16)
                  for w in (w_o_attn, w_o_lru, w_out, w_ffn_gate, w_ffn_up, w_ffn_down))

    xp = x_prompt.reshape(batch * seq, D_MODEL)
    q, k, v, kb, vb, qi, kie, kio, kw, u, ga, gb = _proj(xp, g_mix[layer], w_parts)
    attn = _prompt_attn(rel_bias, q, qi, kw, kb, vb, kie, kio, batch, seq)
    lru, buf_p, h_p = _rglru(u.reshape(batch, seq, D_RNN),
                             jnp.zeros((batch, CONV_W - 1, D_RNN), f32),
                             jnp.zeros((batch, D_RNN), f32), *lru_w)
    y_prompt = _merge_ffn(xp, attn, lru.reshape(batch * seq, D_RNN), ga, gb, out_w,
                          g_ffn[layer], g_final).reshape(batch, seq, D_MODEL)
    new_k_prompt = k.reshape(1, batch, seq, N_KV_HEADS, HEAD_DIM)
    new_v_prompt = v.reshape(1, batch, seq, N_KV_HEADS, HEAD_DIM)
    new_kidx_prompt = kw[:, :IDX_DIM].reshape(1, batch, seq, IDX_DIM)

    xs = x_sample.reshape(n_seq * n_new, D_MODEL)
    q, k, v, kb, vb, qi, kie, kio, kw, u, ga, gb = _proj(xs, g_mix[layer], w_parts)
    attn = _sample_attention(page_table, rel_bias, cache_k[layer], cache_v[layer], cache_kidx[layer],
                             q, qi, kw, kb, vb, n_seq, n_new)
    lru, buf_s, h_s = _rglru(u.reshape(n_seq, n_new, D_RNN), state_conv[layer], state_rnn[layer], *lru_w)
    y_sample = _merge_ffn(xs, attn, lru.reshape(n_seq * n_new, D_RNN), ga, gb, out_w,
                          g_ffn[layer], g_final).reshape(n_seq, n_new, D_MODEL)
    new_k_sample = k.reshape(1, n_seq, n_new, N_KV_HEADS, HEAD_DIM)
    new_v_sample = v.reshape(1, n_seq, n_new, N_KV_HEADS, HEAD_DIM)
    new_kidx_sample = kw[:, :IDX_DIM].reshape(1, n_seq, n_new, IDX_DIM)

    return (y_prompt, y_sample, new_k_prompt, new_v_prompt, new_kidx_prompt, buf_p[None], h_p[None],
            new_k_sample, new_v_sample, new_kidx_sample, buf_s[None], h_s[None])
```

```python
import functools
import math

import jax
import jax.numpy as jnp
from jax import lax
from jax.experimental import pallas as pl
from jax.experimental.pallas import tpu as pltpu

D_MODEL = 1024
N_HEADS = 8
N_KV_HEADS = 2
HEAD_DIM = 128
GROUP = N_HEADS // N_KV_HEADS
IDX_HEADS = 8
IDX_DIM = 64
TOPK_MAX = 256
D_RNN = D_MODEL
LRU_BLOCKS = 8
LRU_BLOCK_W = D_RNN // LRU_BLOCKS
CONV_W = 4
LRU_C = 8.0
N_BUCKETS = 32
MAX_EXACT = N_BUCKETS // 2
MAX_DISTANCE = 128
EPS = 1e-6
NEG_INF = -1e30
PAGE_SIZE = 128
KV_DIM = N_KV_HEADS * HEAD_DIM
LOG2E = math.log2(math.e)
Q_SCALE = HEAD_DIM ** -0.5 * LOG2E
WI_SCALE = IDX_HEADS ** -0.5 * IDX_DIM ** -0.5

LANES = 128
SUBLANES = 8
BF16_SUBLANES = 16
VMEM_LIMIT = 56 * 1024 * 1024

_BUCKET_STEPS = tuple(
    math.ceil(MAX_EXACT * (MAX_DISTANCE / MAX_EXACT) ** (k / (N_BUCKETS - MAX_EXACT)))
    for k in range(1, N_BUCKETS - MAX_EXACT))
M_INIT = -1e29
BOUND_SLACK = 1.0 + 2.0 ** -6
MIN_DENOMINATOR = 2.0 ** -60

_INT_MIN = -2 ** 31
_F32_MANTISSA_MASK = 0x007FFFFF
_INF_KEY = 0x7F800000 - _F32_MANTISSA_MASK
bf16 = jnp.bfloat16
f32 = jnp.float32
i32 = jnp.int32


def _dot(a, b):
    return jnp.dot(a, b, preferred_element_type=f32)


def _dot_nt(a, b):
    return lax.dot_general(a, b, (((1,), (1,)), ((), ())), preferred_element_type=f32)


def _const_spec(shape):
    zeros = (0,) * len(shape)
    return pl.BlockSpec(shape, lambda *_: zeros, pipeline_mode=pl.Buffered(1))


def _smem_spec():
    return pl.BlockSpec(memory_space=pltpu.SMEM)


def _key_to_float(key):
    mag = jnp.minimum(jnp.abs(jnp.maximum(key, -_INF_KEY)), _INF_KEY)
    bits = jnp.where(mag == 0, 0, mag + _F32_MANTISSA_MASK)
    return lax.bitcast_convert_type(jnp.where(key < 0, bits | _INT_MIN, bits), f32)


def _bucket(dist):
    d = jnp.maximum(dist, 0)
    large = jnp.full(d.shape, MAX_EXACT, i32)
    for step in _BUCKET_STEPS:
        large = large + (d >= step).astype(i32)
    return jnp.where(d < MAX_EXACT, d, large)


def _proj_kernel(x_ref, g_ref, wq_ref, wkv_ref, wqi_ref, wke_ref, wko_ref, wkw_ref, wu_ref,
                 wga_ref, wgb_ref,
                 q_ref, k_ref, v_ref, kb_ref, vb_ref, qi_ref, kie_ref, kio_ref, kw_ref,
                 u_ref, ga_ref, gb_ref):
    x = x_ref[...]
    ms = jnp.mean(x * x, axis=-1, keepdims=True)
    xn = ((x * lax.rsqrt(ms + EPS)) * g_ref[...]).astype(bf16)

    q_ref[...] = (_dot(xn, wq_ref[...]) * Q_SCALE).astype(bf16)
    kv = _dot(xn, wkv_ref[...])
    k_ref[...] = kv[:, :KV_DIM]
    v_ref[...] = kv[:, KV_DIM:]
    kb_ref[...] = kv[:, :KV_DIM].astype(bf16)
    vb_ref[...] = kv[:, KV_DIM:].astype(bf16)
    qi_ref[...] = _dot(xn, wqi_ref[...]).astype(bf16)
    kie_ref[...] = _dot(xn, wke_ref[...]).astype(bf16)
    kio_ref[...] = _dot(xn, wko_ref[...]).astype(bf16)
    kw = _dot(xn, wkw_ref[...])
    lane = lax.broadcasted_iota(i32, kw.shape, 1)
    kw_ref[...] = jnp.where(lane >= IDX_DIM, kw * WI_SCALE, kw)
    u_ref[...] = _dot(xn, wu_ref[...])
    ga_ref[...] = _dot(xn, wga_ref[...])
    gb_ref[...] = _dot(xn, wgb_ref[...])


def _split_w_in(w_in):
    sizes = (N_HEADS * HEAD_DIM, KV_DIM, KV_DIM, IDX_HEADS * IDX_DIM, IDX_DIM, IDX_HEADS,
             D_RNN, D_MODEL, D_MODEL)
    parts, acc = [], 0
    for s in sizes:
        parts.append(w_in[:, acc:acc + s])
        acc += s
    wq, wk, wv, wqi, wki, wwi, wu, wga, wgb = parts
    zk = jnp.zeros_like(wki)
    wkv = jnp.concatenate([wk, wv], axis=1)
    wke = jnp.concatenate([wki, zk], axis=1)
    wko = jnp.concatenate([zk, wki], axis=1)
    wkw = jnp.concatenate(
        [wki, wwi, jnp.zeros((w_in.shape[0], LANES - IDX_DIM - IDX_HEADS), w_in.dtype)], axis=1)
    return tuple(w.astype(bf16) for w in (wq, wkv, wqi, wke, wko, wkw, wu, wga, wgb))


def _proj(x2, g_mix, w_parts):
    n = x2.shape[0]
    tm = min(512, n)
    assert n % tm == 0

    def rows(width):
        return pl.BlockSpec((tm, width), lambda i: (i, 0))

    out_widths = (N_HEADS * HEAD_DIM, KV_DIM, KV_DIM, KV_DIM, KV_DIM, IDX_HEADS * IDX_DIM,
                  LANES, LANES, LANES, D_RNN, D_MODEL, D_MODEL)
    out_dtypes = (bf16, f32, f32, bf16, bf16, bf16, bf16, bf16, f32, f32, f32, f32)
    return pl.pallas_call(
        _proj_kernel,
        grid=(n // tm,),
        in_specs=[rows(D_MODEL), _const_spec((1, D_MODEL))] + [_const_spec(w.shape) for w in w_parts],
        out_specs=[rows(w) for w in out_widths],
        out_shape=[jax.ShapeDtypeStruct((n, w), d) for w, d in zip(out_widths, out_dtypes)],
        compiler_params=pltpu.CompilerParams(
            dimension_semantics=("arbitrary",), vmem_limit_bytes=VMEM_LIMIT),
        name="proj",
    )(x2, g_mix.reshape(1, D_MODEL), *w_parts)


def _kth_largest(count_ge, n_all, k):
    def body(it, carry):
        t_u, n_at = carry
        cand_u = t_u | lax.shift_left(jnp.int32(1), jnp.int32(31) - it)
        n = count_ge(_key_to_float(cand_u ^ _INT_MIN))
        ok = n >= k
        return jnp.where(ok, cand_u, t_u), jnp.where(ok, n, n_at)

    t_u, n_at = lax.fori_loop(0, 32, body, (jnp.zeros(n_all.shape, i32), n_all))
    return t_u ^ _INT_MIN, n_at


def _tie_cut(count_eq_below, shape, need, n_bits):
    def body(it, x):
        bit = lax.shift_left(jnp.int32(1), jnp.int32(n_bits - 1) - it)
        cand = x | bit
        cnt = count_eq_below(cand)
        return jnp.where(cnt < need, cand, x)

    return lax.fori_loop(0, n_bits, body, jnp.zeros(shape, i32))


def _prompt_attn_kernel(rb_ref, q_ref, qi_ref, kw_ref, kb_ref, vb_ref, kie_ref, kio_ref, o_ref,
                        score_ref, mask_ref, bias_ref, mx_ref, acc_ref, knorm_ref, *, tq, topk, seq):
    b = pl.program_id(0)
    i = pl.program_id(1)
    tk = tq
    krow = lax.broadcasted_iota(i32, (tk, tq), 0)
    qcol = lax.broadcasted_iota(i32, (tk, tq), 1)

    @pl.when((b == 0) & (i == 0))
    def _():
        for t in range(2):
            bucket = _bucket(t * tq + krow - qcol)
            for h in range(N_HEADS):
                bias_ref[t * N_HEADS + h] = jnp.zeros((tq, tk), f32)

            def fill(bb, carry, t=t, bucket=bucket):
                hit = bucket == bb
                for h in range(N_HEADS):
                    bias_ref[t * N_HEADS + h] = jnp.where(
                        hit, rb_ref[bb, h] * LOG2E, bias_ref[t * N_HEADS + h])
                return carry

            lax.fori_loop(0, N_BUCKETS, fill, 0)

    w_t = kw_ref[...].T

    def score_block(j, causal):
        off = pl.multiple_of(j * tk, tk)
        ke = kie_ref[pl.ds(off, tk), :]
        ko = kio_ref[pl.ds(off, tk), :]
        s = jnp.zeros((tk, tq), f32)
        for p in range(IDX_HEADS // 2):
            rhs = qi_ref[:, p * LANES:(p + 1) * LANES]
            c0 = IDX_DIM + 2 * p
            s = s + jnp.maximum(_dot_nt(ke, rhs), 0.0) * w_t[c0:c0 + 1, :]
            s = s + jnp.maximum(_dot_nt(ko, rhs), 0.0) * w_t[c0 + 1:c0 + 2, :]
        if causal:
            s = jnp.where(krow <= qcol, s, NEG_INF)
        score_ref[pl.ds(off, tk), :] = s

    def score_loop(j, carry):
        score_block(j, False)
        return carry

    lax.fori_loop(0, i, score_loop, 0)
    score_block(i, True)

    n_part = 4

    def score_count(pred):
        def body(j, c):
            off = pl.multiple_of(j * tk, tk)
            hit = pred(score_ref[pl.ds(off, tk), :], off).astype(i32)
            parts = [hit[g * SUBLANES:(g + 1) * SUBLANES] for g in range(n_part)]
            for g in range(n_part, tk // SUBLANES):
                parts[g % n_part] = parts[g % n_part] + hit[g * SUBLANES:(g + 1) * SUBLANES]
            return c + ((parts[0] + parts[1]) + (parts[2] + parts[3]))

        c = lax.fori_loop(0, i + 1, body, jnp.zeros((SUBLANES, tq), i32))
        return jnp.sum(c, axis=0, keepdims=True)

    def count_ge(t):
        return score_count(lambda sc, off: sc >= t)

    thr_key, n_ge = _kth_largest(count_ge, jnp.full((1, tq), (i + 1) * tk, i32), topk)
    thr = _key_to_float(thr_key)

    @pl.when(jnp.max(n_ge) > topk)
    def _():
        need = topk - count_ge(_key_to_float(thr_key + 1))
        cut = _tie_cut(lambda x: score_count(lambda sc, off: (sc == thr) & (krow + off < x)),
                       (1, tq), need, (seq - 1).bit_length())
        below = _key_to_float(thr_key - 1)

        def demote(j, carry):
            off = pl.multiple_of(j * tk, tk)
            sc = score_ref[pl.ds(off, tk), :]
            score_ref[pl.ds(off, tk), :] = jnp.where((sc == thr) & (krow + off > cut), below, sc)
            return carry

        lax.fori_loop(0, i + 1, demote, 0)

    far_bias = [rb_ref[N_BUCKETS - 1, h] * LOG2E for h in range(N_HEADS)]

    def to_mask(j, causal):
        off = pl.multiple_of(j * tk, tk)
        sel = score_ref[pl.ds(off, tk), :] >= thr
        if causal:
            sel = sel & (krow <= qcol)
        mask_ref[:, pl.ds(off, tk)] = jnp.where(sel, 0.0, NEG_INF).T

    def mask_loop(j, carry):
        to_mask(j, False)
        return carry

    lax.fori_loop(0, i, mask_loop, 0)
    to_mask(i, True)

    def group_logits(j, mode):
        off = pl.multiple_of(j * tk, tk)
        mask = mask_ref[:, pl.ds(off, tk)]
        for n in range(N_KV_HEADS):
            ns = slice(n * HEAD_DIM, (n + 1) * HEAD_DIM)
            heads = range(n * GROUP, (n + 1) * GROUP)
            qs = jnp.concatenate([q_ref[:, h * HEAD_DIM:(h + 1) * HEAD_DIM] for h in heads], axis=0)
            s = _dot_nt(qs, kb_ref[pl.ds(off, tk), ns])
            per_head = []
            for g, h in enumerate(heads):
                sg = s[g * tq:(g + 1) * tq] + mask
                if mode != 0:
                    sg = sg + bias_ref[(2 - mode) * N_HEADS + h]
                per_head.append((h, sg))
            yield n, off, ns, per_head

    def over_blocks(block_fn):
        def far(j, carry):
            block_fn(j, 0)
            return carry

        lax.fori_loop(0, i - 1, far, 0)

        @pl.when(i >= 1)
        def _():
            block_fn(i - 1, 1)

        block_fn(i, 2)

    def exact_max():
        mx_ref[...] = jnp.full(mx_ref.shape, NEG_INF, f32)

        def max_block(j, mode):
            for _, _, _, per_head in group_logits(j, mode):
                for h, sg in per_head:
                    part = sg[:, 0:LANES]
                    for c in range(1, tk // LANES):
                        part = jnp.maximum(part, sg[:, c * LANES:(c + 1) * LANES])
                    if mode == 0:
                        part = part + far_bias[h]
                    mx_ref[h] = jnp.maximum(mx_ref[h], part)

        over_blocks(max_block)
        for h in range(N_HEADS):
            mx_ref[h] = jnp.broadcast_to(jnp.max(mx_ref[h], axis=-1, keepdims=True), (tq, LANES))

    ones_cols = jnp.ones((tk, HEAD_DIM), bf16)

    def value_sweep():
        acc_ref[...] = jnp.zeros(acc_ref.shape, f32)

        def pv_block(j, mode):
            for n, off, ns, per_head in group_logits(j, mode):
                ps = []
                for h, sg in per_head:
                    shift = mx_ref[h] - far_bias[h] if mode == 0 else mx_ref[h]
                    shift = jnp.concatenate([shift] * (tk // LANES), axis=1)
                    ps.append(jnp.exp2(sg - shift).astype(bf16))
                values = jnp.concatenate([vb_ref[pl.ds(off, tk), ns], ones_cols], axis=1)
                acc_ref[n] = acc_ref[n] + _dot(jnp.concatenate(ps, axis=0), values)

        over_blocks(pv_block)

    @pl.when(i == 0)
    def _():
        knorm_ref[...] = jnp.zeros(knorm_ref.shape, f32)

    diag = pl.multiple_of(i * tk, tk)
    for n in range(N_KV_HEADS):
        kf = kb_ref[pl.ds(diag, tk), n * HEAD_DIM:(n + 1) * HEAD_DIM].astype(f32)
        ksq = jnp.max(jnp.sum(kf * kf, axis=-1, keepdims=True), axis=0, keepdims=True)
        knorm_ref[n] = jnp.maximum(knorm_ref[n], jnp.broadcast_to(ksq, knorm_ref.shape[1:]))
    for h in range(N_HEADS):
        qf = q_ref[:, h * HEAD_DIM:(h + 1) * HEAD_DIM].astype(f32)
        qsq = jnp.sum(qf * qf, axis=-1, keepdims=True)
        bias_max = lax.fori_loop(1, N_BUCKETS, lambda bb, m, h=h: jnp.maximum(m, rb_ref[bb, h]),
                                 rb_ref[0, h]) * LOG2E
        bound = jnp.sqrt(qsq * knorm_ref[h // GROUP][0:1, 0:1]) * BOUND_SLACK + bias_max
        mx_ref[h] = jnp.broadcast_to(bound, (tq, LANES))
    value_sweep()

    @pl.when(jnp.min(acc_ref[:, :, HEAD_DIM:]) < MIN_DENOMINATOR)
    def _():
        exact_max()
        value_sweep()

    for h in range(N_HEADS):
        acc = acc_ref[h // GROUP, (h % GROUP) * tq:(h % GROUP + 1) * tq, :]
        out = acc[:, :HEAD_DIM] / acc[:, HEAD_DIM:]
        o_ref[:, h * HEAD_DIM:(h + 1) * HEAD_DIM] = out.astype(o_ref.dtype)


def _prompt_attn(rel_bias, q, qi, kw, kb, vb, kie, kio, batch, seq):
    tq = min(256, seq)
    assert seq % tq == 0 and tq % LANES == 0 and tq >= MAX_DISTANCE
    nq = seq // tq
    topk = min(TOPK_MAX, seq // 4)

    def qrows(width):
        return pl.BlockSpec((tq, width), lambda b, i: (b * nq + i, 0))

    def krows(width):
        return pl.BlockSpec((seq, width), lambda b, i: (b, 0))

    kern = functools.partial(_prompt_attn_kernel, tq=tq, topk=topk, seq=seq)
    return pl.pallas_call(
        kern,
        grid=(batch, nq),
        in_specs=[_smem_spec(), qrows(N_HEADS * HEAD_DIM), qrows(IDX_HEADS * IDX_DIM), qrows(LANES),
                  krows(KV_DIM), krows(KV_DIM), krows(LANES), krows(LANES)],
        out_specs=qrows(N_HEADS * HEAD_DIM),
        out_shape=jax.ShapeDtypeStruct((batch * seq, N_HEADS * HEAD_DIM), bf16),
        scratch_shapes=[
            pltpu.VMEM((seq, tq), f32),
            pltpu.VMEM((tq, seq), f32),
            pltpu.VMEM((2 * N_HEADS, tq, tq), f32),
            pltpu.VMEM((N_HEADS, tq, LANES), f32),
            pltpu.VMEM((N_KV_HEADS, GROUP * tq, 2 * HEAD_DIM), f32),
            pltpu.VMEM((N_KV_HEADS, SUBLANES, LANES), f32),
        ],
        compiler_params=pltpu.CompilerParams(
            dimension_semantics=("arbitrary", "arbitrary"), vmem_limit_bytes=VMEM_LIMIT),
        name="prompt_attn",
    )(rel_bias, q, qi, kw, kb, vb, kie, kio)


def _sample_index_kernel(pt_ref, qh_ref, wc_ref, kin_ref, *rest, pages_per_step, n_new):
    del pt_ref
    page_refs = rest[:pages_per_step]
    past_ref, new_ref = rest[pages_per_step:]
    c = pl.program_id(1)
    qh = qh_ref[...]
    wc = wc_ref[...]

    def scores(keys_t):
        s = jnp.maximum(_dot(qh, keys_t), 0.0) * wc
        out = s[0:n_new]
        for h in range(1, IDX_HEADS):
            out = out + s[h * n_new:(h + 1) * n_new]
        return out

    pages = jnp.concatenate([r[...].astype(bf16) for r in page_refs], axis=1)
    past_ref[...] = scores(pages)

    @pl.when(c == 0)
    def _():
        sc = scores(kin_ref[...])
        tok = lax.broadcasted_iota(i32, sc.shape, 0)
        key = lax.broadcasted_iota(i32, sc.shape, 1)
        new_ref[...] = jnp.where(key <= tok, sc, NEG_INF)


def _sample_attn_kernel(pt_ref, rb_ref, past_ref, new_ref, q_ref, kn_ref, vn_ref, *rest,
                        pages_per_step, n_new, n_past, topk):
    del pt_ref
    P = pages_per_step
    k_refs = rest[:P]
    v_refs = rest[P:2 * P]
    o_ref = rest[2 * P]
    score_ref, kcat_ref, vcat_ref, thr_ref, m_ref, l_ref, acc_ref = rest[2 * P + 1:]
    c = pl.program_id(1)
    n_steps = pl.num_programs(1)
    n_rows = N_HEADS * n_new
    half = GROUP * n_new
    step_keys = P * PAGE_SIZE
    n_keys = n_past + LANES

    def head_column(bucket_row):
        r = lax.broadcasted_iota(i32, (n_rows, 1), 0) // n_new
        out = jnp.zeros((n_rows, 1), f32)
        for h in range(N_HEADS):
            out = jnp.where(r == h, rb_ref[bucket_row, h] * LOG2E, out)
        return out

    def bias_tile(dist):
        bucket = _bucket(dist)

        def fill(bb, acc):
            return jnp.where(bucket == bb, head_column(bb), acc)

        return lax.fori_loop(0, N_BUCKETS, fill, jnp.zeros(dist.shape, f32))

    @pl.when(c == 0)
    def _():
        score_ref[:, :n_past] = past_ref[...]
        score_ref[:, n_past:] = new_ref[...]

        def count(pred):
            return jnp.sum(pred(score_ref[...]).astype(i32), axis=-1, keepdims=True)

        def count_ge(t):
            return count(lambda sc: sc >= t)

        thr_key, n_ge = _kth_largest(count_ge, jnp.full((n_new, 1), n_keys, i32), topk)
        thr = _key_to_float(thr_key)
        thr_ref[...] = thr

        @pl.when(jnp.max(n_ge) > topk)
        def _():
            need = topk - count_ge(_key_to_float(thr_key + 1))
            colk = lax.broadcasted_iota(i32, (n_new, n_keys), 1)
            cut = _tie_cut(lambda x: count(lambda sc: (sc == thr) & (colk < x)),
                           (n_new, 1), need, (n_keys - 1).bit_length())
            sc = score_ref[...]
            score_ref[...] = jnp.where((sc == thr) & (colk > cut), _key_to_float(thr_key - 1), sc)

        m_ref[...] = jnp.full(m_ref.shape, M_INIT, f32)
        l_ref[...] = jnp.zeros(l_ref.shape, f32)
        acc_ref[...] = jnp.zeros(acc_ref.shape, f32)

    thr = thr_ref[...]
    far_bias = head_column(N_BUCKETS - 1)
    tok = lax.broadcasted_iota(i32, (n_rows, LANES), 0) % n_new
    kpos = lax.broadcasted_iota(i32, (n_rows, LANES), 1)

    def online_update(n, s, values):
        rows = slice(n * half, (n + 1) * half)
        m_old = m_ref[rows]
        m_new = jnp.maximum(m_old, jnp.max(s, axis=-1, keepdims=True))
        alpha = jnp.exp2(m_old - m_new)
        p = jnp.exp2(s - m_new)
        l_ref[rows] = alpha * l_ref[rows] + jnp.sum(p, axis=-1, keepdims=True)
        acc_ref[rows] = alpha * acc_ref[rows] + _dot(p.astype(bf16), values)
        m_ref[rows] = m_new

    def tiled_mask(scores):
        mask = jnp.where(scores >= thr, 0.0, NEG_INF)
        return jnp.concatenate([mask] * GROUP, axis=0)

    for pp in range(P):
        for n in range(N_KV_HEADS):
            rows = pl.ds(n, PAGE_SIZE, stride=N_KV_HEADS)
            kcat_ref[n, pp * PAGE_SIZE:(pp + 1) * PAGE_SIZE, :] = k_refs[pp][rows, :].astype(bf16)
            vcat_ref[n, pp * PAGE_SIZE:(pp + 1) * PAGE_SIZE, :] = v_refs[pp][rows, :].astype(bf16)

    base = pl.multiple_of(c * step_keys, step_keys)
    mask = tiled_mask(score_ref[:, pl.ds(base, step_keys)])
    last_bias = lax.cond(c == n_steps - 1,
                         lambda: bias_tile(PAGE_SIZE + tok - kpos),
                         lambda: jnp.broadcast_to(far_bias, (n_rows, LANES)))
    for n in range(N_KV_HEADS):
        rows = slice(n * half, (n + 1) * half)
        s = _dot_nt(q_ref[rows, :], kcat_ref[n])
        s = jnp.concatenate([s[:, :step_keys - PAGE_SIZE] + far_bias[rows],
                             s[:, step_keys - PAGE_SIZE:] + last_bias[rows]], axis=1)
        online_update(n, s + mask, vcat_ref[n])

    @pl.when(c == n_steps - 1)
    def _():
        causal = (lax.broadcasted_iota(i32, (half, LANES), 1)
                  <= lax.broadcasted_iota(i32, (half, LANES), 0) % n_new)
        mask = jnp.where(causal, tiled_mask(score_ref[:, n_past:]), NEG_INF)
        bias = bias_tile(tok - kpos)
        for n in range(N_KV_HEADS):
            rows = slice(n * half, (n + 1) * half)
            ns = slice(n * HEAD_DIM, (n + 1) * HEAD_DIM)
            s = _dot_nt(q_ref[rows, :], kn_ref[:, ns]) + bias[rows]
            online_update(n, s + mask, vn_ref[:, ns])
        o_ref[...] = acc_ref[...] / l_ref[...]


def _sample_attention(page_table, rel_bias, cache_k, cache_v, cache_kidx, q, qi, kw, kb, vb, n_seq, n_new):
    n_pages = page_table.shape[1]
    n_pool = cache_k.shape[0]
    n_past = n_pages * PAGE_SIZE
    topk = min(TOPK_MAX, (n_past + n_new) // 4)
    P = min(32, n_pages)
    assert n_pages % P == 0 and n_new == SUBLANES
    n_steps = n_pages // P
    n_rows = N_HEADS * n_new
    page_cols = N_KV_HEADS * PAGE_SIZE

    qh = qi.reshape(n_seq, n_new, IDX_HEADS, IDX_DIM).transpose(0, 2, 1, 3).reshape(n_seq, n_rows, IDX_DIM)
    wc = kw[:, IDX_DIM:IDX_DIM + IDX_HEADS].reshape(n_seq, n_new, IDX_HEADS).transpose(0, 2, 1)
    wc = wc.reshape(n_seq, n_rows, 1)
    kin = kw[:, :IDX_DIM].astype(bf16).reshape(n_seq, n_new, IDX_DIM).transpose(0, 2, 1)
    kin = jnp.pad(kin, ((0, 0), (0, 0), (0, LANES - n_new)))
    qs = q.reshape(n_seq, n_new, N_HEADS, HEAD_DIM).transpose(0, 2, 1, 3).reshape(n_seq, n_rows, HEAD_DIM)
    pad = ((0, 0), (0, LANES - n_new), (0, 0))
    kn = jnp.pad(kb.reshape(n_seq, n_new, KV_DIM), pad)
    vn = jnp.pad(vb.reshape(n_seq, n_new, KV_DIM), pad)
    ck = cache_k.reshape(n_pool, page_cols, HEAD_DIM)
    cv = cache_v.reshape(n_pool, page_cols, HEAD_DIM)
    cki = jnp.swapaxes(cache_kidx, -1, -2)

    def seq_block(shape):
        nd = len(shape)
        return pl.BlockSpec((None,) + shape, lambda b, c, pt: (b,) + (0,) * nd)

    def page_block(shape, pp):
        return pl.BlockSpec((None,) + shape, lambda b, c, pt: (pt[b, c * P + pp], 0, 0))

    past_keys, new_keys = pl.pallas_call(
        functools.partial(_sample_index_kernel, pages_per_step=P, n_new=n_new),
        grid_spec=pltpu.PrefetchScalarGridSpec(
            num_scalar_prefetch=1,
            grid=(n_seq, n_steps),
            in_specs=[seq_block((n_rows, IDX_DIM)), seq_block((n_rows, 1)), seq_block((IDX_DIM, LANES))]
                     + [page_block((IDX_DIM, PAGE_SIZE), pp) for pp in range(P)],
            out_specs=[pl.BlockSpec((None, n_new, P * PAGE_SIZE), lambda b, c, pt: (b, 0, c)),
                       seq_block((n_new, LANES))],
        ),
        out_shape=[jax.ShapeDtypeStruct((n_seq, n_new, n_past), f32),
                   jax.ShapeDtypeStruct((n_seq, n_new, LANES), f32)],
        compiler_params=pltpu.CompilerParams(
            dimension_semantics=("arbitrary", "arbitrary"), vmem_limit_bytes=VMEM_LIMIT),
        name="sample_index",
    )(page_table, qh, wc, kin, *([cki] * P))

    out = pl.pallas_call(
        functools.partial(_sample_attn_kernel, pages_per_step=P, n_new=n_new, n_past=n_past, topk=topk),
        grid_spec=pltpu.PrefetchScalarGridSpec(
            num_scalar_prefetch=1,
            grid=(n_seq, n_steps),
            in_specs=[_smem_spec(), seq_block((n_new, n_past)), seq_block((n_new, LANES)),
                      seq_block((n_rows, HEAD_DIM)), seq_block((LANES, KV_DIM)), seq_block((LANES, KV_DIM))]
                     + [page_block((page_cols, HEAD_DIM), pp) for pp in range(P)]
                     + [page_block((page_cols, HEAD_DIM), pp) for pp in range(P)],
            out_specs=seq_block((n_rows, HEAD_DIM)),
            scratch_shapes=[
                pltpu.VMEM((n_new, n_past + LANES), f32),
                pltpu.VMEM((N_KV_HEADS, P * PAGE_SIZE, HEAD_DIM), bf16),
                pltpu.VMEM((N_KV_HEADS, P * PAGE_SIZE, HEAD_DIM), bf16),
                pltpu.VMEM((n_new, 1), f32),
                pltpu.VMEM((n_rows, 1), f32),
                pltpu.VMEM((n_rows, 1), f32),
                pltpu.VMEM((n_rows, HEAD_DIM), f32),
            ],
        ),
        out_shape=jax.ShapeDtypeStruct((n_seq, n_rows, HEAD_DIM), f32),
        compiler_params=pltpu.CompilerParams(
            dimension_semantics=("arbitrary", "arbitrary"), vmem_limit_bytes=VMEM_LIMIT),
        name="sample_attn",
    )(page_table, rel_bias, past_keys, new_keys, qs, kn, vn, *([ck] * P), *([cv] * P))

    out = out.reshape(n_seq, N_HEADS, n_new, HEAD_DIM).transpose(0, 2, 1, 3)
    return out.reshape(n_seq * n_new, N_HEADS * HEAD_DIM).astype(bf16)


def _rglru_kernel(u_ref, buf_ref, h0_ref, cw_ref, cb_ref, wg_ref, bg_ref, lam_ref,
                  y_ref, nbuf_ref, hT_ref,
                  ext_ref, a_ref, b_ref, hs_ref, h_ref, *, tt):
    t = pl.program_id(1)
    head = SUBLANES

    @pl.when(t == 0)
    def _():
        ext_ref[head - (CONV_W - 1):head, :] = buf_ref[...]
        h_ref[...] = h0_ref[...]

    ext_ref[head:head + tt, :] = u_ref[...]
    xc = cb_ref[...] + ext_ref[head - 3:head - 3 + tt, :] * cw_ref[0:1, :]
    for j in range(1, CONV_W):
        xc = xc + ext_ref[head - 3 + j:head - 3 + j + tt, :] * cw_ref[j:j + 1, :]
    tail = ext_ref[head + tt - (CONV_W - 1):head + tt, :]
    nbuf_ref[...] = tail
    ext_ref[head - (CONV_W - 1):head, :] = tail

    lam = lam_ref[...]
    neg = -lam
    softplus = jnp.maximum(neg, 0.0) + jnp.log1p(jnp.exp(-jnp.abs(neg)))
    for n in range(LRU_BLOCKS):
        ns = slice(n * LRU_BLOCK_W, (n + 1) * LRU_BLOCK_W)
        xn = xc[:, ns]
        gates = _dot(xn.astype(bf16), wg_ref[n]) + bg_ref[n]
        r = jax.nn.sigmoid(gates[:, :LRU_BLOCK_W])
        ig = jax.nn.sigmoid(gates[:, LRU_BLOCK_W:])
        log_a = (-LRU_C) * r * softplus[:, ns]
        a = jnp.exp(log_a)
        a_ref[:, ns] = a
        b_ref[:, ns] = jnp.sqrt(-jnp.tanh(log_a) * (1.0 + a * a)) * (ig * xn)

    def step(r_, h):
        h = a_ref[pl.ds(r_, 1), :] * h + b_ref[pl.ds(r_, 1), :]
        hs_ref[pl.ds(r_, 1), :] = h
        return h

    h = lax.fori_loop(0, tt, step, h_ref[...], unroll=8)
    h_ref[...] = h
    hT_ref[...] = h
    y_ref[...] = hs_ref[...].astype(y_ref.dtype)


def _rglru(u, conv_buf, h0, conv_w, conv_b, w_rg, b_rg, w_ig, b_ig, lam):
    nb, seq, _ = u.shape
    tt = min(256, seq)
    assert seq % tt == 0 and tt % SUBLANES == 0 and tt >= CONV_W - 1
    wg = jnp.concatenate([w_rg, w_ig], axis=-1).astype(bf16)
    bg = jnp.concatenate([b_rg, b_ig], axis=-1).reshape(LRU_BLOCKS, 1, 2 * LRU_BLOCK_W)

    def per_seq(rows):
        return pl.BlockSpec((None, rows, D_RNN), lambda b, t: (b, 0, 0))

    y, nbuf, hT = pl.pallas_call(
        functools.partial(_rglru_kernel, tt=tt),
        grid=(nb, seq // tt),
        in_specs=[pl.BlockSpec((None, tt, D_RNN), lambda b, t: (b, t, 0)),
                  per_seq(CONV_W - 1), per_seq(1),
                  _const_spec((CONV_W, D_RNN)), _const_spec((1, D_RNN)),
                  _const_spec(wg.shape), _const_spec(bg.shape), _const_spec((1, D_RNN))],
        out_specs=[pl.BlockSpec((None, tt, D_RNN), lambda b, t: (b, t, 0)),
                   per_seq(CONV_W - 1), per_seq(1)],
        out_shape=[jax.ShapeDtypeStruct((nb, seq, D_RNN), bf16),
                   jax.ShapeDtypeStruct((nb, CONV_W - 1, D_RNN), f32),
                   jax.ShapeDtypeStruct((nb, 1, D_RNN), f32)],
        scratch_shapes=[
            pltpu.VMEM((SUBLANES + tt, D_RNN), f32),
            pltpu.VMEM((tt, D_RNN), f32),
            pltpu.VMEM((tt, D_RNN), f32),
            pltpu.VMEM((tt, D_RNN), f32),
            pltpu.VMEM((1, D_RNN), f32),
        ],
        compiler_params=pltpu.CompilerParams(
            dimension_semantics=("arbitrary", "arbitrary"), vmem_limit_bytes=VMEM_LIMIT),
        name="rglru",
    )(u, conv_buf, h0.reshape(nb, 1, D_RNN), conv_w, conv_b.reshape(1, D_RNN), wg, bg,
      lam.reshape(1, D_RNN))
    return y, nbuf, hT.reshape(nb, D_RNN)


def _merge_ffn_kernel(x_ref, attn_ref, lru_ref, ga_ref, gb_ref, woa_ref, wol_ref, wout_ref,
                      gf_ref, wfg_ref, wfu_ref, wfd_ref, gfin_ref, y_ref):
    merged = (jax.nn.sigmoid(ga_ref[...]) * _dot(attn_ref[...], woa_ref[...])
              + jax.nn.sigmoid(gb_ref[...]) * _dot(lru_ref[...], wol_ref[...]))
    h = x_ref[...] + _dot(merged.astype(bf16), wout_ref[...])
    hn = (h * lax.rsqrt(jnp.mean(h * h, axis=-1, keepdims=True) + EPS)) * gf_ref[...]
    hn = hn.astype(bf16)
    act = jax.nn.silu(_dot(hn, wfg_ref[...])) * _dot(hn, wfu_ref[...])
    y = h + _dot(act.astype(bf16), wfd_ref[...])
    y_ref[...] = (y * lax.rsqrt(jnp.mean(y * y, axis=-1, keepdims=True) + EPS)) * gfin_ref[...]


def _merge_ffn(x2, attn, lru, ga, gb, weights, g_ffn, g_final):
    n = x2.shape[0]
    tm = min(256, n)
    assert n % tm == 0
    woa, wol, wout, wfg, wfu, wfd = weights

    def rows(width):
        return pl.BlockSpec((tm, width), lambda i: (i, 0))

    return pl.pallas_call(
        _merge_ffn_kernel,
        grid=(n // tm,),
        in_specs=[rows(D_MODEL), rows(N_HEADS * HEAD_DIM), rows(D_RNN), rows(D_MODEL), rows(D_MODEL),
                  _const_spec(woa.shape), _const_spec(wol.shape), _const_spec(wout.shape),
                  _const_spec((1, D_MODEL)), _const_spec(wfg.shape), _const_spec(wfu.shape),
                  _const_spec(wfd.shape), _const_spec((1, D_MODEL))],
        out_specs=rows(D_MODEL),
        out_shape=jax.ShapeDtypeStruct((n, D_MODEL), f32),
        compiler_params=pltpu.CompilerParams(
            dimension_semantics=("arbitrary",), vmem_limit_bytes=VMEM_LIMIT),
        name="merge_ffn",
    )(x2, attn, lru, ga, gb, woa, wol, wout, g_ffn.reshape(1, D_MODEL), wfg, wfu, wfd,
      g_final.reshape(1, D_MODEL))


def kernel(x_prompt, x_sample, cache_k, cache_v, cache_kidx, state_conv, state_rnn, page_table,
           rel_bias, g_mix, w_in, conv_w, conv_b, w_rgate, b_rgate, w_igate, b_igate, lru_lambda,
           w_o_attn, w_o_lru, w_out, g_ffn, w_ffn_gate, w_ffn_up, w_ffn_down, g_final):
    assert w_in.shape[0] == 1, "one trunk layer"
    batch, seq, _ = x_prompt.shape
    n_seq, n_new, _ = x_sample.shape
    layer = 0

    w_parts = _split_w_in(w_in[layer])
    lru_w = (conv_w[layer], conv_b[layer], w_rgate[layer], b_rgate[layer], w_igate[layer],
             b_igate[layer], lru_lambda[layer])
    out_w = tuple(w[layer].astype(bf16)
                  for w in (w_o_attn, w_o_lru, w_out, w_ffn_gate, w_ffn_up, w_ffn_down))

    xp = x_prompt.reshape(batch * seq, D_MODEL)
    q, k, v, kb, vb, qi, kie, kio, kw, u, ga, gb = _proj(xp, g_mix[layer], w_parts)
    attn = _prompt_attn(rel_bias, q, qi, kw, kb, vb, kie, kio, batch, seq)
    lru, buf_p, h_p = _rglru(u.reshape(batch, seq, D_RNN),
                             jnp.zeros((batch, CONV_W - 1, D_RNN), f32),
                             jnp.zeros((batch, D_RNN), f32), *lru_w)
    y_prompt = _merge_ffn(xp, attn, lru.reshape(batch * seq, D_RNN), ga, gb, out_w,
                          g_ffn[layer], g_final).reshape(batch, seq, D_MODEL)
    new_k_prompt = k.reshape(1, batch, seq, N_KV_HEADS, HEAD_DIM)
    new_v_prompt = v.reshape(1, batch, seq, N_KV_HEADS, HEAD_DIM)
    new_kidx_prompt = kw[:, :IDX_DIM].reshape(1, batch, seq, IDX_DIM)

    xs = x_sample.reshape(n_seq * n_new, D_MODEL)
    q, k, v, kb, vb, qi, kie, kio, kw, u, ga, gb = _proj(xs, g_mix[layer], w_parts)
    attn = _sample_attention(page_table, rel_bias, cache_k[layer], cache_v[layer], cache_kidx[layer],
                             q, qi, kw, kb, vb, n_seq, n_new)
    lru, buf_s, h_s = _rglru(u.reshape(n_seq, n_new, D_RNN), state_conv[layer], state_rnn[layer], *lru_w)
    y_sample = _merge_ffn(xs, attn, lru.reshape(n_seq * n_new, D_RNN), ga, gb, out_w,
                          g_ffn[layer], g_final).reshape(n_seq, n_new, D_MODEL)
    new_k_sample = k.reshape(1, n_seq, n_new, N_KV_HEADS, HEAD_DIM)
    new_v_sample = v.reshape(1, n_seq, n_new, N_KV_HEADS, HEAD_DIM)
    new_kidx_sample = kw[:, :IDX_DIM].reshape(1, n_seq, n_new, IDX_DIM)

    return (y_prompt, y_sample, new_k_prompt, new_v_prompt, new_kidx_prompt, buf_p[None], h_p[None],
            new_k_sample, new_v_sample, new_kidx_sample, buf_s[None], h_s[None])
```

```python
import functools
import math

import jax
import jax.numpy as jnp
from jax import lax
from jax.experimental import pallas as pl
from jax.experimental.pallas import tpu as pltpu

D_MODEL = 1024
N_HEADS = 8
N_KV_HEADS = 2
HEAD_DIM = 128
GROUP = N_HEADS // N_KV_HEADS
IDX_HEADS = 8
IDX_DIM = 64
TOPK_MAX = 256
D_RNN = D_MODEL
LRU_BLOCKS = 8
LRU_BLOCK_W = D_RNN // LRU_BLOCKS
CONV_W = 4
LRU_C = 8.0
N_BUCKETS = 32
MAX_EXACT = N_BUCKETS // 2
MAX_DISTANCE = 128
EPS = 1e-6
NEG_INF = -1e30
PAGE_SIZE = 128
KV_DIM = N_KV_HEADS * HEAD_DIM
LOG2E = math.log2(math.e)
Q_SCALE = HEAD_DIM ** -0.5 * LOG2E
WI_SCALE = IDX_HEADS ** -0.5 * IDX_DIM ** -0.5

LANES = 128
SUBLANES = 8
BF16_SUBLANES = 16
VMEM_LIMIT = 56 * 1024 * 1024

_BUCKET_STEPS = tuple(
    math.ceil(MAX_EXACT * (MAX_DISTANCE / MAX_EXACT) ** (k / (N_BUCKETS - MAX_EXACT)))
    for k in range(1, N_BUCKETS - MAX_EXACT))
M_INIT = -1e29
BOUND_SLACK = 1.0 + 2.0 ** -6
MIN_DENOMINATOR = 2.0 ** -60

_INT_MIN = -2 ** 31
_F32_MANTISSA_MASK = 0x007FFFFF
_INF_KEY = 0x7F800000 - _F32_MANTISSA_MASK
_BF16_MANTISSA_MASK = 0x7F
_INF_KEY16 = 0x7F80 - _BF16_MANTISSA_MASK
bf16 = jnp.bfloat16
f32 = jnp.float32
i32 = jnp.int32


def _dot(a, b):
    return jnp.dot(a, b, preferred_element_type=f32)


def _dot_nt(a, b):
    return lax.dot_general(a, b, (((1,), (1,)), ((), ())), preferred_element_type=f32)


def _const_spec(shape):
    zeros = (0,) * len(shape)
    return pl.BlockSpec(shape, lambda *_: zeros, pipeline_mode=pl.Buffered(1))


def _smem_spec():
    return pl.BlockSpec(memory_space=pltpu.SMEM)


def _key_to_float(key):
    mag = jnp.minimum(jnp.abs(jnp.maximum(key, -_INF_KEY)), _INF_KEY)
    bits = jnp.where(mag == 0, 0, mag + _F32_MANTISSA_MASK)
    return lax.bitcast_convert_type(jnp.where(key < 0, bits | _INT_MIN, bits), f32)


def _grid_to_float(key):
    mag = jnp.minimum(jnp.abs(jnp.maximum(key, -_INF_KEY16)), _INF_KEY16)
    bits = lax.shift_left(jnp.where(mag == 0, 0, mag + _BF16_MANTISSA_MASK), 16)
    return lax.bitcast_convert_type(jnp.where(key < 0, bits | _INT_MIN, bits), f32)


def _bucket(dist):
    d = jnp.maximum(dist, 0)
    large = jnp.full(d.shape, MAX_EXACT, i32)
    for step in _BUCKET_STEPS:
        large = large + (d >= step).astype(i32)
    return jnp.where(d < MAX_EXACT, d, large)


def _proj_kernel(x_ref, g_ref, wq_ref, wkv_ref, wqi_ref, wke_ref, wko_ref, wkw_ref, wu_ref,
                 wga_ref, wgb_ref,
                 q_ref, k_ref, v_ref, kb_ref, vb_ref, qi_ref, kie_ref, kio_ref, kw_ref,
                 u_ref, ga_ref, gb_ref):
    x = x_ref[...]
    ms = jnp.mean(x * x, axis=-1, keepdims=True)
    xn = ((x * lax.rsqrt(ms + EPS)) * g_ref[...]).astype(bf16)

    q_ref[...] = (_dot(xn, wq_ref[...]) * Q_SCALE).astype(bf16)
    kv = _dot(xn, wkv_ref[...])
    k_ref[...] = kv[:, :KV_DIM]
    v_ref[...] = kv[:, KV_DIM:]
    kb_ref[...] = kv[:, :KV_DIM].astype(bf16)
    vb_ref[...] = kv[:, KV_DIM:].astype(bf16)
    qi_ref[...] = _dot(xn, wqi_ref[...]).astype(bf16)
    kie_ref[...] = _dot(xn, wke_ref[...]).astype(bf16)
    kio_ref[...] = _dot(xn, wko_ref[...]).astype(bf16)
    kw = _dot(xn, wkw_ref[...])
    lane = lax.broadcasted_iota(i32, kw.shape, 1)
    kw_ref[...] = jnp.where(lane >= IDX_DIM, kw * WI_SCALE, kw)
    u_ref[...] = _dot(xn, wu_ref[...])
    ga_ref[...] = _dot(xn, wga_ref[...])
    gb_ref[...] = _dot(xn, wgb_ref[...])


def _split_w_in(w_in):
    sizes = (N_HEADS * HEAD_DIM, KV_DIM, KV_DIM, IDX_HEADS * IDX_DIM, IDX_DIM, IDX_HEADS,
             D_RNN, D_MODEL, D_MODEL)
    parts, acc = [], 0
    for s in sizes:
        parts.append(w_in[:, acc:acc + s])
        acc += s
    wq, wk, wv, wqi, wki, wwi, wu, wga, wgb = parts
    zk = jnp.zeros_like(wki)
    wkv = jnp.concatenate([wk, wv], axis=1)
    wke = jnp.concatenate([wki, zk], axis=1)
    wko = jnp.concatenate([zk, wki], axis=1)
    wkw = jnp.concatenate(
        [wki, wwi, jnp.zeros((w_in.shape[0], LANES - IDX_DIM - IDX_HEADS), w_in.dtype)], axis=1)
    return tuple(w.astype(bf16) for w in (wq, wkv, wqi, wke, wko, wkw, wu, wga, wgb))


def _proj(x2, g_mix, w_parts):
    n = x2.shape[0]
    tm = min(512, n)
    assert n % tm == 0

    def rows(width):
        return pl.BlockSpec((tm, width), lambda i: (i, 0))

    out_widths = (N_HEADS * HEAD_DIM, KV_DIM, KV_DIM, KV_DIM, KV_DIM, IDX_HEADS * IDX_DIM,
                  LANES, LANES, LANES, D_RNN, D_MODEL, D_MODEL)
    out_dtypes = (bf16, f32, f32, bf16, bf16, bf16, bf16, bf16, f32, f32, f32, f32)
    return pl.pallas_call(
        _proj_kernel,
        grid=(n // tm,),
        in_specs=[rows(D_MODEL), _const_spec((1, D_MODEL))] + [_const_spec(w.shape) for w in w_parts],
        out_specs=[rows(w) for w in out_widths],
        out_shape=[jax.ShapeDtypeStruct((n, w), d) for w, d in zip(out_widths, out_dtypes)],
        compiler_params=pltpu.CompilerParams(
            dimension_semantics=("arbitrary",), vmem_limit_bytes=VMEM_LIMIT),
        name="proj",
    )(x2, g_mix.reshape(1, D_MODEL), *w_parts)


def _greedy_bits(count_at, n_bits, n_start, k):
    def body(it, carry):
        x, n_at = carry
        cand = x | lax.shift_left(jnp.int32(1), jnp.int32(n_bits - 1) - it)
        n = count_at(cand)
        ok = n >= k
        return jnp.where(ok, cand, x), jnp.where(ok, n, n_at)

    return lax.fori_loop(0, n_bits, body, (jnp.zeros(n_start.shape, i32), n_start))


def _kth_largest(count_ge, n_all, k):
    t_u, n_at = _greedy_bits(lambda u: count_ge(_key_to_float(u ^ _INT_MIN)), 32, n_all, k)
    return t_u ^ _INT_MIN, n_at


def _tie_cut(count_eq_below, shape, need, n_bits):
    def body(it, x):
        bit = lax.shift_left(jnp.int32(1), jnp.int32(n_bits - 1) - it)
        cand = x | bit
        cnt = count_eq_below(cand)
        return jnp.where(cnt < need, cand, x)

    return lax.fori_loop(0, n_bits, body, jnp.zeros(shape, i32))


def _prompt_attn_kernel(rb_ref, q_ref, qi_ref, kw_ref, kb_ref, vb_ref, kie_ref, kio_ref, o_ref,
                        score_ref, floor_ref, mask_ref, bias_ref, mx_ref, acc_ref, knorm_ref,
                        *, tq, topk, seq):
    b = pl.program_id(0)
    i = pl.program_id(1)
    tk = tq
    grid = jnp.bfloat16
    krow = lax.broadcasted_iota(i32, (tk, tq), 0)
    qcol = lax.broadcasted_iota(i32, (tk, tq), 1)

    @pl.when((b == 0) & (i == 0))
    def _():
        for t in range(2):
            bucket = _bucket(t * tq + krow - qcol)
            for h in range(N_HEADS):
                bias_ref[t * N_HEADS + h] = jnp.zeros((tq, tk), f32)

            def fill(bb, carry, t=t, bucket=bucket):
                hit = bucket == bb
                for h in range(N_HEADS):
                    bias_ref[t * N_HEADS + h] = jnp.where(
                        hit, rb_ref[bb, h] * LOG2E, bias_ref[t * N_HEADS + h])
                return carry

            lax.fori_loop(0, N_BUCKETS, fill, 0)

    w_t = kw_ref[...].T

    def score_block(j, causal):
        off = pl.multiple_of(j * tk, tk)
        ke = kie_ref[pl.ds(off, tk), :]
        ko = kio_ref[pl.ds(off, tk), :]
        s = jnp.zeros((tk, tq), f32)
        for p in range(IDX_HEADS // 2):
            rhs = qi_ref[:, p * LANES:(p + 1) * LANES]
            c0 = IDX_DIM + 2 * p
            s = s + jnp.maximum(_dot_nt(ke, rhs), 0.0) * w_t[c0:c0 + 1, :]
            s = s + jnp.maximum(_dot_nt(ko, rhs), 0.0) * w_t[c0 + 1:c0 + 2, :]
        if causal:
            s = jnp.where(krow <= qcol, s, NEG_INF)
        score_ref[pl.ds(off, tk), :] = s
        near = s.astype(grid)
        bits = pltpu.bitcast(near, jnp.int16)
        down = pltpu.bitcast(jnp.where(near > 0, bits - 1, bits + 1), grid)
        floor_ref[pl.ds(off, tk), :] = jnp.where(near.astype(f32) > s, down, near)

    def score_loop(j, carry):
        score_block(j, False)
        return carry

    lax.fori_loop(0, i, score_loop, 0)
    score_block(i, True)

    def floor_count(t):
        t = t.astype(grid)

        def body(j, c):
            off = pl.multiple_of(j * tk, tk)
            hit = jnp.where(floor_ref[pl.ds(off, tk), :] >= t, jnp.ones((), jnp.int16), jnp.zeros((), jnp.int16))
            rows = [hit[g * BF16_SUBLANES:(g + 1) * BF16_SUBLANES] for g in range(tk // BF16_SUBLANES)]
            while len(rows) > 1:
                rows = [a + b for a, b in zip(rows[0::2], rows[1::2])]
            return c + rows[0]

        c = lax.fori_loop(0, i + 1, body, jnp.zeros((BF16_SUBLANES, tq), jnp.int16))
        return jnp.sum(c.astype(i32), axis=0, keepdims=True)


    def score_count(pred):
        def body(j, c):
            off = pl.multiple_of(j * tk, tk)
            hit = pred(score_ref[pl.ds(off, tk), :], off).astype(i32)
            rows = [hit[g * SUBLANES:(g + 1) * SUBLANES] for g in range(tk // SUBLANES)]
            while len(rows) > 1:
                rows = [a + b for a, b in zip(rows[0::2], rows[1::2])]
            return c + rows[0]

        c = lax.fori_loop(0, i + 1, body, jnp.zeros((SUBLANES, tq), i32))
        return jnp.sum(c, axis=0, keepdims=True)

    def count_ge(t):
        return score_count(lambda sc, off: sc >= t)

    n_all = jnp.full((1, tq), (i + 1) * tk, i32)
    lo_key, n_lo = _greedy_bits(lambda key: floor_count(_grid_to_float(key - 2 ** 15)), 16, n_all, topk)
    lo = _grid_to_float(lo_key - 2 ** 15)
    hi = _grid_to_float(lo_key - 2 ** 15 + 1)
    width = hi - lo
    step = jnp.where(width < jnp.inf, width, 0.0) * 2.0 ** -16
    pos, n_ge = _greedy_bits(lambda j: count_ge(lo + j.astype(f32) * step), 16, n_lo, topk)
    thr = lo + pos.astype(f32) * step

    @pl.when(jnp.max(n_ge) > topk)
    def _():
        need = topk - score_count(lambda sc, off: sc > thr)
        cut = _tie_cut(lambda x: score_count(lambda sc, off: (sc == thr) & (krow + off < x)),
                       (1, tq), need, (seq - 1).bit_length())

        def demote(j, carry):
            off = pl.multiple_of(j * tk, tk)
            sc = score_ref[pl.ds(off, tk), :]
            score_ref[pl.ds(off, tk), :] = jnp.where((sc == thr) & (krow + off > cut), -jnp.inf, sc)
            return carry

        lax.fori_loop(0, i + 1, demote, 0)

    far_bias = [rb_ref[N_BUCKETS - 1, h] * LOG2E for h in range(N_HEADS)]

    def to_mask(j, causal):
        off = pl.multiple_of(j * tk, tk)
        sel = score_ref[pl.ds(off, tk), :] >= thr
        if causal:
            sel = sel & (krow <= qcol)
        mask_ref[:, pl.ds(off, tk)] = jnp.where(sel, 0.0, NEG_INF).T

    def mask_loop(j, carry):
        to_mask(j, False)
        return carry

    lax.fori_loop(0, i, mask_loop, 0)
    to_mask(i, True)

    def group_logits(j, mode):
        off = pl.multiple_of(j * tk, tk)
        mask = mask_ref[:, pl.ds(off, tk)]
        for n in range(N_KV_HEADS):
            ns = slice(n * HEAD_DIM, (n + 1) * HEAD_DIM)
            heads = range(n * GROUP, (n + 1) * GROUP)
            qs = jnp.concatenate([q_ref[:, h * HEAD_DIM:(h + 1) * HEAD_DIM] for h in heads], axis=0)
            s = _dot_nt(qs, kb_ref[pl.ds(off, tk), ns])
            per_head = []
            for g, h in enumerate(heads):
                sg = s[g * tq:(g + 1) * tq] + mask
                if mode != 0:
                    sg = sg + bias_ref[(2 - mode) * N_HEADS + h]
                per_head.append((h, sg))
            yield n, off, ns, per_head

    def over_blocks(block_fn):
        def far(j, carry):
            block_fn(j, 0)
            return carry

        lax.fori_loop(0, i - 1, far, 0)

        @pl.when(i >= 1)
        def _():
            block_fn(i - 1, 1)

        block_fn(i, 2)

    def exact_max():
        mx_ref[...] = jnp.full(mx_ref.shape, NEG_INF, f32)

        def max_block(j, mode):
            for _, _, _, per_head in group_logits(j, mode):
                for h, sg in per_head:
                    part = sg[:, 0:LANES]
                    for c in range(1, tk // LANES):
                        part = jnp.maximum(part, sg[:, c * LANES:(c + 1) * LANES])
                    if mode == 0:
                        part = part + far_bias[h]
                    mx_ref[h] = jnp.maximum(mx_ref[h], part)

        over_blocks(max_block)
        for h in range(N_HEADS):
            mx_ref[h] = jnp.broadcast_to(jnp.max(mx_ref[h], axis=-1, keepdims=True), (tq, LANES))

    ones_cols = jnp.ones((tk, HEAD_DIM), bf16)

    def value_sweep():
        acc_ref[...] = jnp.zeros(acc_ref.shape, f32)

        def pv_block(j, mode):
            for n, off, ns, per_head in group_logits(j, mode):
                ps = []
                for h, sg in per_head:
                    shift = mx_ref[h] - far_bias[h] if mode == 0 else mx_ref[h]
                    shift = jnp.concatenate([shift] * (tk // LANES), axis=1)
                    ps.append(jnp.exp2(sg - shift).astype(bf16))
                values = jnp.concatenate([vb_ref[pl.ds(off, tk), ns], ones_cols], axis=1)
                acc_ref[n] = acc_ref[n] + _dot(jnp.concatenate(ps, axis=0), values)

        over_blocks(pv_block)

    @pl.when(i == 0)
    def _():
        knorm_ref[...] = jnp.zeros(knorm_ref.shape, f32)

    diag = pl.multiple_of(i * tk, tk)
    for n in range(N_KV_HEADS):
        kf = kb_ref[pl.ds(diag, tk), n * HEAD_DIM:(n + 1) * HEAD_DIM].astype(f32)
        ksq = jnp.max(jnp.sum(kf * kf, axis=-1, keepdims=True), axis=0, keepdims=True)
        knorm_ref[n] = jnp.maximum(knorm_ref[n], jnp.broadcast_to(ksq, knorm_ref.shape[1:]))
    for h in range(N_HEADS):
        qf = q_ref[:, h * HEAD_DIM:(h + 1) * HEAD_DIM].astype(f32)
        qsq = jnp.sum(qf * qf, axis=-1, keepdims=True)
        bias_max = lax.fori_loop(1, N_BUCKETS, lambda bb, m, h=h: jnp.maximum(m, rb_ref[bb, h]),
                                 rb_ref[0, h]) * LOG2E
        bound = jnp.sqrt(qsq * knorm_ref[h // GROUP][0:1, 0:1]) * BOUND_SLACK + bias_max
        mx_ref[h] = jnp.broadcast_to(bound, (tq, LANES))
    value_sweep()

    @pl.when(jnp.min(acc_ref[:, :, HEAD_DIM:]) < MIN_DENOMINATOR)
    def _():
        exact_max()
        value_sweep()

    for h in range(N_HEADS):
        acc = acc_ref[h // GROUP, (h % GROUP) * tq:(h % GROUP + 1) * tq, :]
        out = acc[:, :HEAD_DIM] / acc[:, HEAD_DIM:]
        o_ref[:, h * HEAD_DIM:(h + 1) * HEAD_DIM] = out.astype(o_ref.dtype)


def _prompt_attn(rel_bias, q, qi, kw, kb, vb, kie, kio, batch, seq):
    tq = min(256, seq)
    assert seq % tq == 0 and tq % LANES == 0 and tq >= MAX_DISTANCE
    nq = seq // tq
    topk = min(TOPK_MAX, seq // 4)

    def qrows(width):
        return pl.BlockSpec((tq, width), lambda b, i: (b * nq + i, 0))

    def krows(width):
        return pl.BlockSpec((seq, width), lambda b, i: (b, 0))

    kern = functools.partial(_prompt_attn_kernel, tq=tq, topk=topk, seq=seq)
    return pl.pallas_call(
        kern,
        grid=(batch, nq),
        in_specs=[_smem_spec(), qrows(N_HEADS * HEAD_DIM), qrows(IDX_HEADS * IDX_DIM), qrows(LANES),
                  krows(KV_DIM), krows(KV_DIM), krows(LANES), krows(LANES)],
        out_specs=qrows(N_HEADS * HEAD_DIM),
        out_shape=jax.ShapeDtypeStruct((batch * seq, N_HEADS * HEAD_DIM), bf16),
        scratch_shapes=[
            pltpu.VMEM((seq, tq), f32),
            pltpu.VMEM((seq, tq), jnp.bfloat16),
            pltpu.VMEM((tq, seq), f32),
            pltpu.VMEM((2 * N_HEADS, tq, tq), f32),
            pltpu.VMEM((N_HEADS, tq, LANES), f32),
            pltpu.VMEM((N_KV_HEADS, GROUP * tq, 2 * HEAD_DIM), f32),
            pltpu.VMEM((N_KV_HEADS, SUBLANES, LANES), f32),
        ],
        compiler_params=pltpu.CompilerParams(
            dimension_semantics=("arbitrary", "arbitrary"), vmem_limit_bytes=VMEM_LIMIT),
        name="prompt_attn",
    )(rel_bias, q, qi, kw, kb, vb, kie, kio)


def _sample_index_kernel(pt_ref, qh_ref, wc_ref, kin_ref, *rest, pages_per_step, n_new):
    del pt_ref
    page_refs = rest[:pages_per_step]
    past_ref, new_ref = rest[pages_per_step:]
    c = pl.program_id(1)
    qh = qh_ref[...]
    wc = wc_ref[...]

    def scores(keys_t):
        s = jnp.maximum(_dot(qh, keys_t), 0.0) * wc
        out = s[0:n_new]
        for h in range(1, IDX_HEADS):
            out = out + s[h * n_new:(h + 1) * n_new]
        return out

    pages = jnp.concatenate([r[...].astype(bf16) for r in page_refs], axis=1)
    past_ref[...] = scores(pages)

    @pl.when(c == 0)
    def _():
        sc = scores(kin_ref[...])
        tok = lax.broadcasted_iota(i32, sc.shape, 0)
        key = lax.broadcasted_iota(i32, sc.shape, 1)
        new_ref[...] = jnp.where(key <= tok, sc, NEG_INF)


def _sample_attn_kernel(pt_ref, rb_ref, past_ref, new_ref, q_ref, kn_ref, vn_ref, *rest,
                        pages_per_step, n_new, n_past, topk):
    del pt_ref
    P = pages_per_step
    k_refs = rest[:P]
    v_refs = rest[P:2 * P]
    o_ref = rest[2 * P]
    score_ref, kcat_ref, vcat_ref, thr_ref, m_ref, l_ref, acc_ref = rest[2 * P + 1:]
    c = pl.program_id(1)
    n_steps = pl.num_programs(1)
    n_rows = N_HEADS * n_new
    half = GROUP * n_new
    step_keys = P * PAGE_SIZE
    n_keys = n_past + LANES

    def head_column(bucket_row):
        r = lax.broadcasted_iota(i32, (n_rows, 1), 0) // n_new
        out = jnp.zeros((n_rows, 1), f32)
        for h in range(N_HEADS):
            out = jnp.where(r == h, rb_ref[bucket_row, h] * LOG2E, out)
        return out

    def bias_tile(dist):
        bucket = _bucket(dist)

        def fill(bb, acc):
            return jnp.where(bucket == bb, head_column(bb), acc)

        return lax.fori_loop(0, N_BUCKETS, fill, jnp.zeros(dist.shape, f32))

    @pl.when(c == 0)
    def _():
        score_ref[:, :n_past] = past_ref[...]
        score_ref[:, n_past:] = new_ref[...]

        def count(pred):
            return jnp.sum(pred(score_ref[...]).astype(i32), axis=-1, keepdims=True)

        def count_ge(t):
            return count(lambda sc: sc >= t)

        thr_key, n_ge = _kth_largest(count_ge, jnp.full((n_new, 1), n_keys, i32), topk)
        thr = _key_to_float(thr_key)
        thr_ref[...] = thr

        @pl.when(jnp.max(n_ge) > topk)
        def _():
            need = topk - count_ge(_key_to_float(thr_key + 1))
            colk = lax.broadcasted_iota(i32, (n_new, n_keys), 1)
            cut = _tie_cut(lambda x: count(lambda sc: (sc == thr) & (colk < x)),
                           (n_new, 1), need, (n_keys - 1).bit_length())
            sc = score_ref[...]
            score_ref[...] = jnp.where((sc == thr) & (colk > cut), _key_to_float(thr_key - 1), sc)

        m_ref[...] = jnp.full(m_ref.shape, M_INIT, f32)
        l_ref[...] = jnp.zeros(l_ref.shape, f32)
        acc_ref[...] = jnp.zeros(acc_ref.shape, f32)

    thr = thr_ref[...]
    far_bias = head_column(N_BUCKETS - 1)
    tok = lax.broadcasted_iota(i32, (n_rows, LANES), 0) % n_new
    kpos = lax.broadcasted_iota(i32, (n_rows, LANES), 1)

    def online_update(n, s, values):
        rows = slice(n * half, (n + 1) * half)
        m_old = m_ref[rows]
        m_new = jnp.maximum(m_old, jnp.max(s, axis=-1, keepdims=True))
        alpha = jnp.exp2(m_old - m_new)
        p = jnp.exp2(s - m_new)
        l_ref[rows] = alpha * l_ref[rows] + jnp.sum(p, axis=-1, keepdims=True)
        acc_ref[rows] = alpha * acc_ref[rows] + _dot(p.astype(bf16), values)
        m_ref[rows] = m_new

    def tiled_mask(scores):
        mask = jnp.where(scores >= thr, 0.0, NEG_INF)
        return jnp.concatenate([mask] * GROUP, axis=0)

    for pp in range(P):
        for n in range(N_KV_HEADS):
            rows = pl.ds(n, PAGE_SIZE, stride=N_KV_HEADS)
            kcat_ref[n, pp * PAGE_SIZE:(pp + 1) * PAGE_SIZE, :] = k_refs[pp][rows, :].astype(bf16)
            vcat_ref[n, pp * PAGE_SIZE:(pp + 1) * PAGE_SIZE, :] = v_refs[pp][rows, :].astype(bf16)

    base = pl.multiple_of(c * step_keys, step_keys)
    mask = tiled_mask(score_ref[:, pl.ds(base, step_keys)])
    last_bias = lax.cond(c == n_steps - 1,
                         lambda: bias_tile(PAGE_SIZE + tok - kpos),
                         lambda: jnp.broadcast_to(far_bias, (n_rows, LANES)))
    for n in range(N_KV_HEADS):
        rows = slice(n * half, (n + 1) * half)
        s = _dot_nt(q_ref[rows, :], kcat_ref[n])
        s = jnp.concatenate([s[:, :step_keys - PAGE_SIZE] + far_bias[rows],
                             s[:, step_keys - PAGE_SIZE:] + last_bias[rows]], axis=1)
        online_update(n, s + mask, vcat_ref[n])

    @pl.when(c == n_steps - 1)
    def _():
        causal = (lax.broadcasted_iota(i32, (half, LANES), 1)
                  <= lax.broadcasted_iota(i32, (half, LANES), 0) % n_new)
        mask = jnp.where(causal, tiled_mask(score_ref[:, n_past:]), NEG_INF)
        bias = bias_tile(tok - kpos)
        for n in range(N_KV_HEADS):
            rows = slice(n * half, (n + 1) * half)
            ns = slice(n * HEAD_DIM, (n + 1) * HEAD_DIM)
            s = _dot_nt(q_ref[rows, :], kn_ref[:, ns]) + bias[rows]
            online_update(n, s + mask, vn_ref[:, ns])
        o_ref[...] = acc_ref[...] / l_ref[...]


def _sample_attention(page_table, rel_bias, cache_k, cache_v, cache_kidx, q, qi, kw, kb, vb, n_seq, n_new):
    n_pages = page_table.shape[1]
    n_pool = cache_k.shape[0]
    n_past = n_pages * PAGE_SIZE
    topk = min(TOPK_MAX, (n_past + n_new) // 4)
    P = min(32, n_pages)
    assert n_pages % P == 0 and n_new == SUBLANES
    n_steps = n_pages // P
    n_rows = N_HEADS * n_new
    page_cols = N_KV_HEADS * PAGE_SIZE

    qh = qi.reshape(n_seq, n_new, IDX_HEADS, IDX_DIM).transpose(0, 2, 1, 3).reshape(n_seq, n_rows, IDX_DIM)
    wc = kw[:, IDX_DIM:IDX_DIM + IDX_HEADS].reshape(n_seq, n_new, IDX_HEADS).transpose(0, 2, 1)
    wc = wc.reshape(n_seq, n_rows, 1)
    kin = kw[:, :IDX_DIM].astype(bf16).reshape(n_seq, n_new, IDX_DIM).transpose(0, 2, 1)
    kin = jnp.pad(kin, ((0, 0), (0, 0), (0, LANES - n_new)))
    qs = q.reshape(n_seq, n_new, N_HEADS, HEAD_DIM).transpose(0, 2, 1, 3).reshape(n_seq, n_rows, HEAD_DIM)
    pad = ((0, 0), (0, LANES - n_new), (0, 0))
    kn = jnp.pad(kb.reshape(n_seq, n_new, KV_DIM), pad)
    vn = jnp.pad(vb.reshape(n_seq, n_new, KV_DIM), pad)
    ck = cache_k.reshape(n_pool, page_cols, HEAD_DIM)
    cv = cache_v.reshape(n_pool, page_cols, HEAD_DIM)
    cki = jnp.swapaxes(cache_kidx, -1, -2)

    def seq_block(shape):
        nd = len(shape)
        return pl.BlockSpec((None,) + shape, lambda b, c, pt: (b,) + (0,) * nd)

    def page_block(shape, pp):
        return pl.BlockSpec((None,) + shape, lambda b, c, pt: (pt[b, c * P + pp], 0, 0))

    past_keys, new_keys = pl.pallas_call(
        functools.partial(_sample_index_kernel, pages_per_step=P, n_new=n_new),
        grid_spec=pltpu.PrefetchScalarGridSpec(
            num_scalar_prefetch=1,
            grid=(n_seq, n_steps),
            in_specs=[seq_block((n_rows, IDX_DIM)), seq_block((n_rows, 1)), seq_block((IDX_DIM, LANES))]
                     + [page_block((IDX_DIM, PAGE_SIZE), pp) for pp in range(P)],
            out_specs=[pl.BlockSpec((None, n_new, P * PAGE_SIZE), lambda b, c, pt: (b, 0, c)),
                       seq_block((n_new, LANES))],
        ),
        out_shape=[jax.ShapeDtypeStruct((n_seq, n_new, n_past), f32),
                   jax.ShapeDtypeStruct((n_seq, n_new, LANES), f32)],
        compiler_params=pltpu.CompilerParams(
            dimension_semantics=("arbitrary", "arbitrary"), vmem_limit_bytes=VMEM_LIMIT),
        name="sample_index",
    )(page_table, qh, wc, kin, *([cki] * P))

    out = pl.pallas_call(
        functools.partial(_sample_attn_kernel, pages_per_step=P, n_new=n_new, n_past=n_past, topk=topk),
        grid_spec=pltpu.PrefetchScalarGridSpec(
            num_scalar_prefetch=1,
            grid=(n_seq, n_steps),
            in_specs=[_smem_spec(), seq_block((n_new, n_past)), seq_block((n_new, LANES)),
                      seq_block((n_rows, HEAD_DIM)), seq_block((LANES, KV_DIM)), seq_block((LANES, KV_DIM))]
                     + [page_block((page_cols, HEAD_DIM), pp) for pp in range(P)]
                     + [page_block((page_cols, HEAD_DIM), pp) for pp in range(P)],
            out_specs=seq_block((n_rows, HEAD_DIM)),
            scratch_shapes=[
                pltpu.VMEM((n_new, n_past + LANES), f32),
                pltpu.VMEM((N_KV_HEADS, P * PAGE_SIZE, HEAD_DIM), bf16),
                pltpu.VMEM((N_KV_HEADS, P * PAGE_SIZE, HEAD_DIM), bf16),
                pltpu.VMEM((n_new, 1), f32),
                pltpu.VMEM((n_rows, 1), f32),
                pltpu.VMEM((n_rows, 1), f32),
                pltpu.VMEM((n_rows, HEAD_DIM), f32),
            ],
        ),
        out_shape=jax.ShapeDtypeStruct((n_seq, n_rows, HEAD_DIM), f32),
        compiler_params=pltpu.CompilerParams(
            dimension_semantics=("arbitrary", "arbitrary"), vmem_limit_bytes=VMEM_LIMIT),
        name="sample_attn",
    )(page_table, rel_bias, past_keys, new_keys, qs, kn, vn, *([ck] * P), *([cv] * P))

    out = out.reshape(n_seq, N_HEADS, n_new, HEAD_DIM).transpose(0, 2, 1, 3)
    return out.reshape(n_seq * n_new, N_HEADS * HEAD_DIM).astype(bf16)


def _rglru_kernel(u_ref, buf_ref, h0_ref, cw_ref, cb_ref, wg_ref, bg_ref, lam_ref,
                  y_ref, nbuf_ref, hT_ref,
                  ext_ref, a_ref, b_ref, hs_ref, h_ref, *, tt):
    t = pl.program_id(1)
    head = SUBLANES

    @pl.when(t == 0)
    def _():
        ext_ref[head - (CONV_W - 1):head, :] = buf_ref[...]
        h_ref[...] = h0_ref[...]

    ext_ref[head:head + tt, :] = u_ref[...]
    xc = cb_ref[...] + ext_ref[head - 3:head - 3 + tt, :] * cw_ref[0:1, :]
    for j in range(1, CONV_W):
        xc = xc + ext_ref[head - 3 + j:head - 3 + j + tt, :] * cw_ref[j:j + 1, :]
    tail = ext_ref[head + tt - (CONV_W - 1):head + tt, :]
    nbuf_ref[...] = tail
    ext_ref[head - (CONV_W - 1):head, :] = tail

    lam = lam_ref[...]
    neg = -lam
    softplus = jnp.maximum(neg, 0.0) + jnp.log1p(jnp.exp(-jnp.abs(neg)))
    for n in range(LRU_BLOCKS):
        ns = slice(n * LRU_BLOCK_W, (n + 1) * LRU_BLOCK_W)
        xn = xc[:, ns]
        gates = _dot(xn.astype(bf16), wg_ref[n]) + bg_ref[n]
        r = jax.nn.sigmoid(gates[:, :LRU_BLOCK_W])
        ig = jax.nn.sigmoid(gates[:, LRU_BLOCK_W:])
        log_a = (-LRU_C) * r * softplus[:, ns]
        a = jnp.exp(log_a)
        a_ref[:, ns] = a
        b_ref[:, ns] = jnp.sqrt(-jnp.tanh(log_a) * (1.0 + a * a)) * (ig * xn)

    def step(r_, h):
        h = a_ref[pl.ds(r_, 1), :] * h + b_ref[pl.ds(r_, 1), :]
        hs_ref[pl.ds(r_, 1), :] = h
        return h

    h = lax.fori_loop(0, tt, step, h_ref[...], unroll=8)
    h_ref[...] = h
    hT_ref[...] = h
    y_ref[...] = hs_ref[...].astype(y_ref.dtype)


def _rglru(u, conv_buf, h0, conv_w, conv_b, w_rg, b_rg, w_ig, b_ig, lam):
    nb, seq, _ = u.shape
    tt = min(256, seq)
    assert seq % tt == 0 and tt % SUBLANES == 0 and tt >= CONV_W - 1
    wg = jnp.concatenate([w_rg, w_ig], axis=-1).astype(bf16)
    bg = jnp.concatenate([b_rg, b_ig], axis=-1).reshape(LRU_BLOCKS, 1, 2 * LRU_BLOCK_W)

    def per_seq(rows):
        return pl.BlockSpec((None, rows, D_RNN), lambda b, t: (b, 0, 0))

    y, nbuf, hT = pl.pallas_call(
        functools.partial(_rglru_kernel, tt=tt),
        grid=(nb, seq // tt),
        in_specs=[pl.BlockSpec((None, tt, D_RNN), lambda b, t: (b, t, 0)),
                  per_seq(CONV_W - 1), per_seq(1),
                  _const_spec((CONV_W, D_RNN)), _const_spec((1, D_RNN)),
                  _const_spec(wg.shape), _const_spec(bg.shape), _const_spec((1, D_RNN))],
        out_specs=[pl.BlockSpec((None, tt, D_RNN), lambda b, t: (b, t, 0)),
                   per_seq(CONV_W - 1), per_seq(1)],
        out_shape=[jax.ShapeDtypeStruct((nb, seq, D_RNN), bf16),
                   jax.ShapeDtypeStruct((nb, CONV_W - 1, D_RNN), f32),
                   jax.ShapeDtypeStruct((nb, 1, D_RNN), f32)],
        scratch_shapes=[
            pltpu.VMEM((SUBLANES + tt, D_RNN), f32),
            pltpu.VMEM((tt, D_RNN), f32),
            pltpu.VMEM((tt, D_RNN), f32),
            pltpu.VMEM((tt, D_RNN), f32),
            pltpu.VMEM((1, D_RNN), f32),
        ],
        compiler_params=pltpu.CompilerParams(
            dimension_semantics=("arbitrary", "arbitrary"), vmem_limit_bytes=VMEM_LIMIT),
        name="rglru",
    )(u, conv_buf, h0.reshape(nb, 1, D_RNN), conv_w, conv_b.reshape(1, D_RNN), wg, bg,
      lam.reshape(1, D_RNN))
    return y, nbuf, hT.reshape(nb, D_RNN)


def _merge_ffn_kernel(x_ref, attn_ref, lru_ref, ga_ref, gb_ref, woa_ref, wol_ref, wout_ref,
                      gf_ref, wfg_ref, wfu_ref, wfd_ref, gfin_ref, y_ref):
    merged = (jax.nn.sigmoid(ga_ref[...]) * _dot(attn_ref[...], woa_ref[...])
              + jax.nn.sigmoid(gb_ref[...]) * _dot(lru_ref[...], wol_ref[...]))
    h = x_ref[...] + _dot(merged.astype(bf16), wout_ref[...])
    hn = (h * lax.rsqrt(jnp.mean(h * h, axis=-1, keepdims=True) + EPS)) * gf_ref[...]
    hn = hn.astype(bf16)
    act = jax.nn.silu(_dot(hn, wfg_ref[...])) * _dot(hn, wfu_ref[...])
    y = h + _dot(act.astype(bf16), wfd_ref[...])
    y_ref[...] = (y * lax.rsqrt(jnp.mean(y * y, axis=-1, keepdims=True) + EPS)) * gfin_ref[...]


def _merge_ffn(x2, attn, lru, ga, gb, weights, g_ffn, g_final):
    n = x2.shape[0]
    tm = min(256, n)
    assert n % tm == 0
    woa, wol, wout, wfg, wfu, wfd = weights

    def rows(width):
        return pl.BlockSpec((tm, width), lambda i: (i, 0))

    return pl.pallas_call(
        _merge_ffn_kernel,
        grid=(n // tm,),
        in_specs=[rows(D_MODEL), rows(N_HEADS * HEAD_DIM), rows(D_RNN), rows(D_MODEL), rows(D_MODEL),
                  _const_spec(woa.shape), _const_spec(wol.shape), _const_spec(wout.shape),
                  _const_spec((1, D_MODEL)), _const_spec(wfg.shape), _const_spec(wfu.shape),
                  _const_spec(wfd.shape), _const_spec((1, D_MODEL))],
        out_specs=rows(D_MODEL),
        out_shape=jax.ShapeDtypeStruct((n, D_MODEL), f32),
        compiler_params=pltpu.CompilerParams(
            dimension_semantics=("arbitrary",), vmem_limit_bytes=VMEM_LIMIT),
        name="merge_ffn",
    )(x2, attn, lru, ga, gb, woa, wol, wout, g_ffn.reshape(1, D_MODEL), wfg, wfu, wfd,
      g_final.reshape(1, D_MODEL))


def kernel(x_prompt, x_sample, cache_k, cache_v, cache_kidx, state_conv, state_rnn, page_table,
           rel_bias, g_mix, w_in, conv_w, conv_b, w_rgate, b_rgate, w_igate, b_igate, lru_lambda,
           w_o_attn, w_o_lru, w_out, g_ffn, w_ffn_gate, w_ffn_up, w_ffn_down, g_final):
    assert w_in.shape[0] == 1, "one trunk layer"
    batch, seq, _ = x_prompt.shape
    n_seq, n_new, _ = x_sample.shape
    layer = 0

    w_parts = _split_w_in(w_in[layer])
    lru_w = (conv_w[layer], conv_b[layer], w_rgate[layer], b_rgate[layer], w_igate[layer],
             b_igate[layer], lru_lambda[layer])
    out_w = tuple(w[layer].astype(bf16)
                  for w in (w_o_attn, w_o_lru, w_out, w_ffn_gate, w_ffn_up, w_ffn_down))

    xp = x_prompt.reshape(batch * seq, D_MODEL)
    q, k, v, kb, vb, qi, kie, kio, kw, u, ga, gb = _proj(xp, g_mix[layer], w_parts)
    attn = _prompt_attn(rel_bias, q, qi, kw, kb, vb, kie, kio, batch, seq)
    lru, buf_p, h_p = _rglru(u.reshape(batch, seq, D_RNN),
                             jnp.zeros((batch, CONV_W - 1, D_RNN), f32),
                             jnp.zeros((batch, D_RNN), f32), *lru_w)
    y_prompt = _merge_ffn(xp, attn, lru.reshape(batch * seq, D_RNN), ga, gb, out_w,
                          g_ffn[layer], g_final).reshape(batch, seq, D_MODEL)
    new_k_prompt = k.reshape(1, batch, seq, N_KV_HEADS, HEAD_DIM)
    new_v_prompt = v.reshape(1, batch, seq, N_KV_HEADS, HEAD_DIM)
    new_kidx_prompt = kw[:, :IDX_DIM].reshape(1, batch, seq, IDX_DIM)

    xs = x_sample.reshape(n_seq * n_new, D_MODEL)
    q, k, v, kb, vb, qi, kie, kio, kw, u, ga, gb = _proj(xs, g_mix[layer], w_parts)
    attn = _sample_attention(page_table, rel_bias, cache_k[layer], cache_v[layer], cache_kidx[layer],
                             q, qi, kw, kb, vb, n_seq, n_new)
    lru, buf_s, h_s = _rglru(u.reshape(n_seq, n_new, D_RNN), state_conv[layer], state_rnn[layer], *lru_w)
    y_sample = _merge_ffn(xs, attn, lru.reshape(n_seq * n_new, D_RNN), ga, gb, out_w,
                          g_ffn[layer], g_final).reshape(n_seq, n_new, D_MODEL)
    new_k_sample = k.reshape(1, n_seq, n_new, N_KV_HEADS, HEAD_DIM)
    new_v_sample = v.reshape(1, n_seq, n_new, N_KV_HEADS, HEAD_DIM)
    new_kidx_sample = kw[:, :IDX_DIM].reshape(1, n_seq, n_new, IDX_DIM)

    return (y_prompt, y_sample, new_k_prompt, new_v_prompt, new_kidx_prompt, buf_p[None], h_p[None],
            new_k_sample, new_v_sample, new_kidx_sample, buf_s[None], h_s[None])
```

```python
import functools
import math

import jax
import jax.numpy as jnp
from jax import lax
from jax.experimental import pallas as pl
from jax.experimental.pallas import tpu as pltpu

D_MODEL = 1024
N_HEADS = 8
N_KV_HEADS = 2
HEAD_DIM = 128
GROUP = N_HEADS // N_KV_HEADS
IDX_HEADS = 8
IDX_DIM = 64
TOPK_MAX = 256
D_RNN = D_MODEL
LRU_BLOCKS = 8
LRU_BLOCK_W = D_RNN // LRU_BLOCKS
CONV_W = 4
LRU_C = 8.0
N_BUCKETS = 32
MAX_EXACT = N_BUCKETS // 2
MAX_DISTANCE = 128
EPS = 1e-6
NEG_INF = -1e30
PAGE_SIZE = 128
KV_DIM = N_KV_HEADS * HEAD_DIM
LOG2E = math.log2(math.e)
Q_SCALE = HEAD_DIM ** -0.5 * LOG2E
WI_SCALE = IDX_HEADS ** -0.5 * IDX_DIM ** -0.5

LANES = 128
SUBLANES = 8
BF16_SUBLANES = 16
VMEM_LIMIT = 56 * 1024 * 1024

_BUCKET_STEPS = tuple(
    math.ceil(MAX_EXACT * (MAX_DISTANCE / MAX_EXACT) ** (k / (N_BUCKETS - MAX_EXACT)))
    for k in range(1, N_BUCKETS - MAX_EXACT))
M_INIT = -1e29
BOUND_SLACK = 1.0 + 2.0 ** -6
MIN_DENOMINATOR = 2.0 ** -60

_INT_MIN = -2 ** 31
_F32_MANTISSA_MASK = 0x007FFFFF
_INF_KEY = 0x7F800000 - _F32_MANTISSA_MASK
_BF16_MANTISSA_MASK = 0x7F
_INF_KEY16 = 0x7F80 - _BF16_MANTISSA_MASK
bf16 = jnp.bfloat16
f32 = jnp.float32
i32 = jnp.int32


def _dot(a, b):
    return jnp.dot(a, b, preferred_element_type=f32)


def _dot_nt(a, b):
    return lax.dot_general(a, b, (((1,), (1,)), ((), ())), preferred_element_type=f32)


def _const_spec(shape):
    zeros = (0,) * len(shape)
    return pl.BlockSpec(shape, lambda *_: zeros, pipeline_mode=pl.Buffered(1))


def _smem_spec():
    return pl.BlockSpec(memory_space=pltpu.SMEM)


def _key_to_float(key):
    mag = jnp.minimum(jnp.abs(jnp.maximum(key, -_INF_KEY)), _INF_KEY)
    bits = jnp.where(mag == 0, 0, mag + _F32_MANTISSA_MASK)
    return lax.bitcast_convert_type(jnp.where(key < 0, bits | _INT_MIN, bits), f32)


def _grid_to_float(key):
    mag = jnp.minimum(jnp.abs(jnp.maximum(key, -_INF_KEY16)), _INF_KEY16)
    bits = lax.shift_left(jnp.where(mag == 0, 0, mag + _BF16_MANTISSA_MASK), 16)
    return lax.bitcast_convert_type(jnp.where(key < 0, bits | _INT_MIN, bits), f32)


def _bucket(dist):
    d = jnp.maximum(dist, 0)
    large = jnp.full(d.shape, MAX_EXACT, i32)
    for step in _BUCKET_STEPS:
        large = large + (d >= step).astype(i32)
    return jnp.where(d < MAX_EXACT, d, large)


def _proj_kernel(x_ref, g_ref, wq_ref, wkv_ref, wqi_ref, wke_ref, wko_ref, wkw_ref, wu_ref,
                 wga_ref, wgb_ref,
                 q_ref, k_ref, v_ref, kb_ref, vb_ref, qi_ref, kie_ref, kio_ref, kw_ref,
                 u_ref, ga_ref, gb_ref):
    x = x_ref[...]
    ms = jnp.mean(x * x, axis=-1, keepdims=True)
    xn = ((x * lax.rsqrt(ms + EPS)) * g_ref[...]).astype(bf16)

    q_ref[...] = (_dot(xn, wq_ref[...]) * Q_SCALE).astype(bf16)
    kv = _dot(xn, wkv_ref[...])
    tm = x.shape[0]
    for n in range(N_KV_HEADS):
        rows = pl.ds(n, tm, stride=N_KV_HEADS)
        k_ref[rows, :] = kv[:, n * HEAD_DIM:(n + 1) * HEAD_DIM]
        v_ref[rows, :] = kv[:, KV_DIM + n * HEAD_DIM:KV_DIM + (n + 1) * HEAD_DIM]
    kb_ref[...] = kv[:, :KV_DIM].astype(bf16)
    vb_ref[...] = kv[:, KV_DIM:].astype(bf16)
    qi_ref[...] = _dot(xn, wqi_ref[...]).astype(bf16)
    kie_ref[...] = _dot(xn, wke_ref[...]).astype(bf16)
    kio_ref[...] = _dot(xn, wko_ref[...]).astype(bf16)
    kw = _dot(xn, wkw_ref[...])
    lane = lax.broadcasted_iota(i32, kw.shape, 1)
    kw_ref[...] = jnp.where(lane >= IDX_DIM, kw * WI_SCALE, kw)
    u_ref[...] = _dot(xn, wu_ref[...])
    ga_ref[...] = _dot(xn, wga_ref[...])
    gb_ref[...] = _dot(xn, wgb_ref[...])


def _split_w_in(w_in):
    sizes = (N_HEADS * HEAD_DIM, KV_DIM, KV_DIM, IDX_HEADS * IDX_DIM, IDX_DIM, IDX_HEADS,
             D_RNN, D_MODEL, D_MODEL)
    parts, acc = [], 0
    for s in sizes:
        parts.append(w_in[:, acc:acc + s])
        acc += s
    wq, wk, wv, wqi, wki, wwi, wu, wga, wgb = parts
    zk = jnp.zeros_like(wki)
    wkv = jnp.concatenate([wk, wv], axis=1)
    wke = jnp.concatenate([wki, zk], axis=1)
    wko = jnp.concatenate([zk, wki], axis=1)
    wkw = jnp.concatenate(
        [wki, wwi, jnp.zeros((w_in.shape[0], LANES - IDX_DIM - IDX_HEADS), w_in.dtype)], axis=1)
    return tuple(w.astype(bf16) for w in (wq, wkv, wqi, wke, wko, wkw, wu, wga, wgb))


def _proj(x2, g_mix, w_parts):
    n = x2.shape[0]
    tm = min(512, n)
    assert n % tm == 0

    def rows(width):
        return pl.BlockSpec((tm, width), lambda i: (i, 0))

    outs = ((1, N_HEADS * HEAD_DIM, bf16), (N_KV_HEADS, HEAD_DIM, f32), (N_KV_HEADS, HEAD_DIM, f32),
            (1, KV_DIM, bf16), (1, KV_DIM, bf16), (1, IDX_HEADS * IDX_DIM, bf16), (1, LANES, bf16),
            (1, LANES, bf16), (1, LANES, f32), (1, D_RNN, f32), (1, D_MODEL, f32), (1, D_MODEL, f32))
    return pl.pallas_call(
        _proj_kernel,
        grid=(n // tm,),
        in_specs=[rows(D_MODEL), _const_spec((1, D_MODEL))] + [_const_spec(w.shape) for w in w_parts],
        out_specs=[pl.BlockSpec((r * tm, w), lambda i: (i, 0)) for r, w, _ in outs],
        out_shape=[jax.ShapeDtypeStruct((r * n, w), d) for r, w, d in outs],
        compiler_params=pltpu.CompilerParams(
            dimension_semantics=("arbitrary",), vmem_limit_bytes=VMEM_LIMIT),
        name="proj",
    )(x2, g_mix.reshape(1, D_MODEL), *w_parts)


def _greedy_bits(count_at, n_bits, n_start, k):
    def body(it, carry):
        x, n_at = carry
        cand = x | lax.shift_left(jnp.int32(1), jnp.int32(n_bits - 1) - it)
        n = count_at(cand)
        ok = n >= k
        return jnp.where(ok, cand, x), jnp.where(ok, n, n_at)

    return lax.fori_loop(0, n_bits, body, (jnp.zeros(n_start.shape, i32), n_start))


def _kth_largest(count_ge, n_all, k):
    t_u, n_at = _greedy_bits(lambda u: count_ge(_key_to_float(u ^ _INT_MIN)), 32, n_all, k)
    return t_u ^ _INT_MIN, n_at


def _tie_cut(count_eq_below, shape, need, n_bits):
    def body(it, x):
        bit = lax.shift_left(jnp.int32(1), jnp.int32(n_bits - 1) - it)
        cand = x | bit
        cnt = count_eq_below(cand)
        return jnp.where(cnt < need, cand, x)

    return lax.fori_loop(0, n_bits, body, jnp.zeros(shape, i32))


def _prompt_attn_kernel(rb_ref, q_ref, qi_ref, kw_ref, kb_ref, vb_ref, kie_ref, kio_ref, o_ref,
                        score_ref, floor_ref, mask_ref, bias_ref, mx_ref, acc_ref, knorm_ref,
                        *, tq, topk, seq):
    b = pl.program_id(0)
    i = pl.program_id(1)
    tk = tq
    grid = jnp.bfloat16
    krow = lax.broadcasted_iota(i32, (tk, tq), 0)
    qcol = lax.broadcasted_iota(i32, (tk, tq), 1)

    @pl.when((b == 0) & (i == 0))
    def _():
        for t in range(2):
            bucket = _bucket(t * tq + krow - qcol)
            for h in range(N_HEADS):
                bias_ref[t * N_HEADS + h] = jnp.zeros((tq, tk), f32)

            def fill(bb, carry, t=t, bucket=bucket):
                hit = bucket == bb
                for h in range(N_HEADS):
                    bias_ref[t * N_HEADS + h] = jnp.where(
                        hit, rb_ref[bb, h] * LOG2E, bias_ref[t * N_HEADS + h])
                return carry

            lax.fori_loop(0, N_BUCKETS, fill, 0)

    w_t = kw_ref[...].T

    def score_block(j, causal):
        off = pl.multiple_of(j * tk, tk)
        ke = kie_ref[pl.ds(off, tk), :]
        ko = kio_ref[pl.ds(off, tk), :]
        s = jnp.zeros((tk, tq), f32)
        for p in range(IDX_HEADS // 2):
            rhs = qi_ref[:, p * LANES:(p + 1) * LANES]
            c0 = IDX_DIM + 2 * p
            s = s + jnp.maximum(_dot_nt(ke, rhs), 0.0) * w_t[c0:c0 + 1, :]
            s = s + jnp.maximum(_dot_nt(ko, rhs), 0.0) * w_t[c0 + 1:c0 + 2, :]
        if causal:
            s = jnp.where(krow <= qcol, s, NEG_INF)
        score_ref[pl.ds(off, tk), :] = s
        near = s.astype(grid)
        bits = pltpu.bitcast(near, jnp.int16)
        down = pltpu.bitcast(jnp.where(near > 0, bits - 1, bits + 1), grid)
        floor_ref[pl.ds(off, tk), :] = jnp.where(near.astype(f32) > s, down, near)

    def two_per_trip(n, fn):
        n = jnp.maximum(n, 0)

        def pair(p, carry):
            fn(2 * p)
            fn(2 * p + 1)
            return carry

        lax.fori_loop(0, n // 2, pair, 0)

        @pl.when(n % 2 == 1)
        def _():
            fn(n - 1)

    two_per_trip(i, lambda j: score_block(j, False))
    score_block(i, True)

    def floor_count(t):
        t = t.astype(grid)

        def body(j, c):
            off = pl.multiple_of(j * tk, tk)
            hit = jnp.where(floor_ref[pl.ds(off, tk), :] >= t, jnp.ones((), jnp.int16), jnp.zeros((), jnp.int16))
            rows = [hit[g * BF16_SUBLANES:(g + 1) * BF16_SUBLANES] for g in range(tk // BF16_SUBLANES)]
            while len(rows) > 1:
                rows = [a + b for a, b in zip(rows[0::2], rows[1::2])]
            return c + rows[0]

        c = lax.fori_loop(0, i + 1, body, jnp.zeros((BF16_SUBLANES, tq), jnp.int16))
        return jnp.sum(c.astype(i32), axis=0, keepdims=True)


    def score_count(pred):
        def body(j, c):
            off = pl.multiple_of(j * tk, tk)
            hit = pred(score_ref[pl.ds(off, tk), :], off).astype(i32)
            rows = [hit[g * SUBLANES:(g + 1) * SUBLANES] for g in range(tk // SUBLANES)]
            while len(rows) > 1:
                rows = [a + b for a, b in zip(rows[0::2], rows[1::2])]
            return c + rows[0]

        c = lax.fori_loop(0, i + 1, body, jnp.zeros((SUBLANES, tq), i32))
        return jnp.sum(c, axis=0, keepdims=True)

    def count_ge(t):
        return score_count(lambda sc, off: sc >= t)

    n_all = jnp.full((1, tq), (i + 1) * tk, i32)
    lo_key, n_lo = _greedy_bits(lambda key: floor_count(_grid_to_float(key - 2 ** 15)), 16, n_all, topk)
    lo = _grid_to_float(lo_key - 2 ** 15)
    hi = _grid_to_float(lo_key - 2 ** 15 + 1)
    width = hi - lo
    step = jnp.where(width < jnp.inf, width, 0.0) * 2.0 ** -16
    pos, n_ge = _greedy_bits(lambda j: count_ge(lo + j.astype(f32) * step), 16, n_lo, topk)
    thr = lo + pos.astype(f32) * step

    @pl.when(jnp.max(n_ge) > topk)
    def _():
        need = topk - score_count(lambda sc, off: sc > thr)
        cut = _tie_cut(lambda x: score_count(lambda sc, off: (sc == thr) & (krow + off < x)),
                       (1, tq), need, (seq - 1).bit_length())

        def demote(j, carry):
            off = pl.multiple_of(j * tk, tk)
            sc = score_ref[pl.ds(off, tk), :]
            score_ref[pl.ds(off, tk), :] = jnp.where((sc == thr) & (krow + off > cut), -jnp.inf, sc)
            return carry

        lax.fori_loop(0, i + 1, demote, 0)

    far_bias = [rb_ref[N_BUCKETS - 1, h] * LOG2E for h in range(N_HEADS)]

    def to_mask(j, causal):
        off = pl.multiple_of(j * tk, tk)
        sel = score_ref[pl.ds(off, tk), :] >= thr
        if causal:
            sel = sel & (krow <= qcol)
        mask_ref[:, pl.ds(off, tk)] = jnp.where(sel, 0.0, NEG_INF).T

    def mask_loop(j, carry):
        to_mask(j, False)
        return carry

    lax.fori_loop(0, i, mask_loop, 0)
    to_mask(i, True)

    def group_logits(j, mode):
        off = pl.multiple_of(j * tk, tk)
        mask = mask_ref[:, pl.ds(off, tk)]
        for n in range(N_KV_HEADS):
            ns = slice(n * HEAD_DIM, (n + 1) * HEAD_DIM)
            heads = range(n * GROUP, (n + 1) * GROUP)
            qs = jnp.concatenate([q_ref[:, h * HEAD_DIM:(h + 1) * HEAD_DIM] for h in heads], axis=0)
            s = _dot_nt(qs, kb_ref[pl.ds(off, tk), ns])
            per_head = []
            for g, h in enumerate(heads):
                sg = s[g * tq:(g + 1) * tq] + mask
                if mode != 0:
                    sg = sg + bias_ref[(2 - mode) * N_HEADS + h]
                per_head.append((h, sg))
            yield n, off, ns, per_head

    def over_blocks(block_fn):
        two_per_trip(i - 1, lambda j: block_fn(j, 0))

        @pl.when(i >= 1)
        def _():
            block_fn(i - 1, 1)

        block_fn(i, 2)

    def exact_max():
        mx_ref[...] = jnp.full(mx_ref.shape, NEG_INF, f32)

        def max_block(j, mode):
            for _, _, _, per_head in group_logits(j, mode):
                for h, sg in per_head:
                    part = sg[:, 0:LANES]
                    for c in range(1, tk // LANES):
                        part = jnp.maximum(part, sg[:, c * LANES:(c + 1) * LANES])
                    if mode == 0:
                        part = part + far_bias[h]
                    mx_ref[h] = jnp.maximum(mx_ref[h], part)

        over_blocks(max_block)
        for h in range(N_HEADS):
            mx_ref[h] = jnp.broadcast_to(jnp.max(mx_ref[h], axis=-1, keepdims=True), (tq, LANES))

    ones_cols = jnp.ones((tk, HEAD_DIM), bf16)

    def value_sweep():
        acc_ref[...] = jnp.zeros(acc_ref.shape, f32)

        def pv_block(j, mode):
            for n, off, ns, per_head in group_logits(j, mode):
                ps = []
                for h, sg in per_head:
                    shift = mx_ref[h] - far_bias[h] if mode == 0 else mx_ref[h]
                    shift = jnp.concatenate([shift] * (tk // LANES), axis=1)
                    ps.append(jnp.exp2(sg - shift).astype(bf16))
                values = jnp.concatenate([vb_ref[pl.ds(off, tk), ns], ones_cols], axis=1)
                acc_ref[n] = acc_ref[n] + _dot(jnp.concatenate(ps, axis=0), values)

        over_blocks(pv_block)

    @pl.when(i == 0)
    def _():
        knorm_ref[...] = jnp.zeros(knorm_ref.shape, f32)

    diag = pl.multiple_of(i * tk, tk)
    for n in range(N_KV_HEADS):
        kf = kb_ref[pl.ds(diag, tk), n * HEAD_DIM:(n + 1) * HEAD_DIM].astype(f32)
        ksq = jnp.max(jnp.sum(kf * kf, axis=-1, keepdims=True), axis=0, keepdims=True)
        knorm_ref[n] = jnp.maximum(knorm_ref[n], jnp.broadcast_to(ksq, knorm_ref.shape[1:]))
    for h in range(N_HEADS):
        qf = q_ref[:, h * HEAD_DIM:(h + 1) * HEAD_DIM].astype(f32)
        qsq = jnp.sum(qf * qf, axis=-1, keepdims=True)
        bias_max = lax.fori_loop(1, N_BUCKETS, lambda bb, m, h=h: jnp.maximum(m, rb_ref[bb, h]),
                                 rb_ref[0, h]) * LOG2E
        bound = jnp.sqrt(qsq * knorm_ref[h // GROUP][0:1, 0:1]) * BOUND_SLACK + bias_max
        mx_ref[h] = jnp.broadcast_to(bound, (tq, LANES))
    value_sweep()

    @pl.when(jnp.min(acc_ref[:, :, HEAD_DIM:]) < MIN_DENOMINATOR)
    def _():
        exact_max()
        value_sweep()

    for h in range(N_HEADS):
        acc = acc_ref[h // GROUP, (h % GROUP) * tq:(h % GROUP + 1) * tq, :]
        out = acc[:, :HEAD_DIM] / acc[:, HEAD_DIM:]
        o_ref[:, h * HEAD_DIM:(h + 1) * HEAD_DIM] = out.astype(o_ref.dtype)


def _prompt_attn(rel_bias, q, qi, kw, kb, vb, kie, kio, batch, seq):
    tq = min(256, seq)
    assert seq % tq == 0 and tq % LANES == 0 and tq >= MAX_DISTANCE
    nq = seq // tq
    topk = min(TOPK_MAX, seq // 4)

    def qrows(width):
        return pl.BlockSpec((tq, width), lambda b, i: (b * nq + i, 0))

    def krows(width):
        return pl.BlockSpec((seq, width), lambda b, i: (b, 0))

    kern = functools.partial(_prompt_attn_kernel, tq=tq, topk=topk, seq=seq)
    return pl.pallas_call(
        kern,
        grid=(batch, nq),
        in_specs=[_smem_spec(), qrows(N_HEADS * HEAD_DIM), qrows(IDX_HEADS * IDX_DIM), qrows(LANES),
                  krows(KV_DIM), krows(KV_DIM), krows(LANES), krows(LANES)],
        out_specs=qrows(N_HEADS * HEAD_DIM),
        out_shape=jax.ShapeDtypeStruct((batch * seq, N_HEADS * HEAD_DIM), bf16),
        scratch_shapes=[
            pltpu.VMEM((seq, tq), f32),
            pltpu.VMEM((seq, tq), jnp.bfloat16),
            pltpu.VMEM((tq, seq), f32),
            pltpu.VMEM((2 * N_HEADS, tq, tq), f32),
            pltpu.VMEM((N_HEADS, tq, LANES), f32),
            pltpu.VMEM((N_KV_HEADS, GROUP * tq, 2 * HEAD_DIM), f32),
            pltpu.VMEM((N_KV_HEADS, SUBLANES, LANES), f32),
        ],
        compiler_params=pltpu.CompilerParams(
            dimension_semantics=("arbitrary", "arbitrary"), vmem_limit_bytes=VMEM_LIMIT),
        name="prompt_attn",
    )(rel_bias, q, qi, kw, kb, vb, kie, kio)


def _sample_index_kernel(pt_ref, qh_ref, wc_ref, kin_ref, *rest, pages_per_step, n_new):
    del pt_ref
    page_refs = rest[:pages_per_step]
    past_ref, new_ref = rest[pages_per_step:]
    c = pl.program_id(1)
    qh = qh_ref[...]
    wc = wc_ref[...]

    def scores(keys_t):
        s = jnp.maximum(_dot(qh, keys_t), 0.0) * wc
        out = s[0:n_new]
        for h in range(1, IDX_HEADS):
            out = out + s[h * n_new:(h + 1) * n_new]
        return out

    pages = jnp.concatenate([r[...].astype(bf16) for r in page_refs], axis=1)
    past_ref[...] = scores(pages)

    @pl.when(c == 0)
    def _():
        sc = scores(kin_ref[...])
        tok = lax.broadcasted_iota(i32, sc.shape, 0)
        key = lax.broadcasted_iota(i32, sc.shape, 1)
        new_ref[...] = jnp.where(key <= tok, sc, NEG_INF)


def _sample_attn_kernel(pt_ref, rb_ref, past_ref, new_ref, q_ref, kn_ref, vn_ref, *rest,
                        pages_per_step, n_new, n_past, topk):
    del pt_ref
    P = pages_per_step
    k_refs = rest[:P]
    v_refs = rest[P:2 * P]
    o_ref = rest[2 * P]
    score_ref, kcat_ref, vcat_ref, thr_ref, m_ref, l_ref, acc_ref = rest[2 * P + 1:]
    c = pl.program_id(1)
    n_steps = pl.num_programs(1)
    n_rows = N_HEADS * n_new
    half = GROUP * n_new
    step_keys = P * PAGE_SIZE
    n_keys = n_past + LANES

    def head_column(bucket_row):
        r = lax.broadcasted_iota(i32, (n_rows, 1), 0) // n_new
        out = jnp.zeros((n_rows, 1), f32)
        for h in range(N_HEADS):
            out = jnp.where(r == h, rb_ref[bucket_row, h] * LOG2E, out)
        return out

    def bias_tile(dist):
        bucket = _bucket(dist)

        def fill(bb, acc):
            return jnp.where(bucket == bb, head_column(bb), acc)

        return lax.fori_loop(0, N_BUCKETS, fill, jnp.zeros(dist.shape, f32))

    @pl.when(c == 0)
    def _():
        score_ref[:, :n_past] = past_ref[...]
        score_ref[:, n_past:] = new_ref[...]

        def count(pred):
            return jnp.sum(pred(score_ref[...]).astype(i32), axis=-1, keepdims=True)

        def count_ge(t):
            return count(lambda sc: sc >= t)

        thr_key, n_ge = _kth_largest(count_ge, jnp.full((n_new, 1), n_keys, i32), topk)
        thr = _key_to_float(thr_key)
        thr_ref[...] = thr

        @pl.when(jnp.max(n_ge) > topk)
        def _():
            need = topk - count_ge(_key_to_float(thr_key + 1))
            colk = lax.broadcasted_iota(i32, (n_new, n_keys), 1)
            cut = _tie_cut(lambda x: count(lambda sc: (sc == thr) & (colk < x)),
                           (n_new, 1), need, (n_keys - 1).bit_length())
            sc = score_ref[...]
            score_ref[...] = jnp.where((sc == thr) & (colk > cut), _key_to_float(thr_key - 1), sc)

        m_ref[...] = jnp.full(m_ref.shape, M_INIT, f32)
        l_ref[...] = jnp.zeros(l_ref.shape, f32)
        acc_ref[...] = jnp.zeros(acc_ref.shape, f32)

    thr = thr_ref[...]
    far_bias = head_column(N_BUCKETS - 1)
    tok = lax.broadcasted_iota(i32, (n_rows, LANES), 0) % n_new
    kpos = lax.broadcasted_iota(i32, (n_rows, LANES), 1)

    def online_update(n, s, values):
        rows = slice(n * half, (n + 1) * half)
        m_old = m_ref[rows]
        m_new = jnp.maximum(m_old, jnp.max(s, axis=-1, keepdims=True))
        alpha = jnp.exp2(m_old - m_new)
        p = jnp.exp2(s - m_new)
        l_ref[rows] = alpha * l_ref[rows] + jnp.sum(p, axis=-1, keepdims=True)
        acc_ref[rows] = alpha * acc_ref[rows] + _dot(p.astype(bf16), values)
        m_ref[rows] = m_new

    def tiled_mask(scores):
        mask = jnp.where(scores >= thr, 0.0, NEG_INF)
        return jnp.concatenate([mask] * GROUP, axis=0)

    for pp in range(P):
        for n in range(N_KV_HEADS):
            rows = pl.ds(n, PAGE_SIZE, stride=N_KV_HEADS)
            kcat_ref[n, pp * PAGE_SIZE:(pp + 1) * PAGE_SIZE, :] = k_refs[pp][rows, :].astype(bf16)
            vcat_ref[n, pp * PAGE_SIZE:(pp + 1) * PAGE_SIZE, :] = v_refs[pp][rows, :].astype(bf16)

    base = pl.multiple_of(c * step_keys, step_keys)
    mask = tiled_mask(score_ref[:, pl.ds(base, step_keys)])
    last_bias = lax.cond(c == n_steps - 1,
                         lambda: bias_tile(PAGE_SIZE + tok - kpos),
                         lambda: jnp.broadcast_to(far_bias, (n_rows, LANES)))
    for n in range(N_KV_HEADS):
        rows = slice(n * half, (n + 1) * half)
        s = _dot_nt(q_ref[rows, :], kcat_ref[n])
        s = jnp.concatenate([s[:, :step_keys - PAGE_SIZE] + far_bias[rows],
                             s[:, step_keys - PAGE_SIZE:] + last_bias[rows]], axis=1)
        online_update(n, s + mask, vcat_ref[n])

    @pl.when(c == n_steps - 1)
    def _():
        causal = (lax.broadcasted_iota(i32, (half, LANES), 1)
                  <= lax.broadcasted_iota(i32, (half, LANES), 0) % n_new)
        mask = jnp.where(causal, tiled_mask(score_ref[:, n_past:]), NEG_INF)
        bias = bias_tile(tok - kpos)
        for n in range(N_KV_HEADS):
            rows = slice(n * half, (n + 1) * half)
            ns = slice(n * HEAD_DIM, (n + 1) * HEAD_DIM)
            s = _dot_nt(q_ref[rows, :], kn_ref[:, ns]) + bias[rows]
            online_update(n, s + mask, vn_ref[:, ns])
        o_ref[...] = acc_ref[...] / l_ref[...]


def _sample_attention(page_table, rel_bias, cache_k, cache_v, cache_kidx, q, qi, kw, kb, vb, n_seq, n_new):
    n_pages = page_table.shape[1]
    n_pool = cache_k.shape[0]
    n_past = n_pages * PAGE_SIZE
    topk = min(TOPK_MAX, (n_past + n_new) // 4)
    P = min(32, n_pages)
    assert n_pages % P == 0 and n_new == SUBLANES
    n_steps = n_pages // P
    n_rows = N_HEADS * n_new
    page_cols = N_KV_HEADS * PAGE_SIZE

    qh = qi.reshape(n_seq, n_new, IDX_HEADS, IDX_DIM).transpose(0, 2, 1, 3).reshape(n_seq, n_rows, IDX_DIM)
    wc = kw[:, IDX_DIM:IDX_DIM + IDX_HEADS].reshape(n_seq, n_new, IDX_HEADS).transpose(0, 2, 1)
    wc = wc.reshape(n_seq, n_rows, 1)
    kin = kw[:, :IDX_DIM].astype(bf16).reshape(n_seq, n_new, IDX_DIM).transpose(0, 2, 1)
    kin = jnp.pad(kin, ((0, 0), (0, 0), (0, LANES - n_new)))
    qs = q.reshape(n_seq, n_new, N_HEADS, HEAD_DIM).transpose(0, 2, 1, 3).reshape(n_seq, n_rows, HEAD_DIM)
    pad = ((0, 0), (0, LANES - n_new), (0, 0))
    kn = jnp.pad(kb.reshape(n_seq, n_new, KV_DIM), pad)
    vn = jnp.pad(vb.reshape(n_seq, n_new, KV_DIM), pad)
    ck = cache_k.reshape(n_pool, page_cols, HEAD_DIM)
    cv = cache_v.reshape(n_pool, page_cols, HEAD_DIM)
    cki = jnp.swapaxes(cache_kidx, -1, -2)

    def seq_block(shape):
        nd = len(shape)
        return pl.BlockSpec((None,) + shape, lambda b, c, pt: (b,) + (0,) * nd)

    def page_block(shape, pp):
        return pl.BlockSpec((None,) + shape, lambda b, c, pt: (pt[b, c * P + pp], 0, 0))

    past_keys, new_keys = pl.pallas_call(
        functools.partial(_sample_index_kernel, pages_per_step=P, n_new=n_new),
        grid_spec=pltpu.PrefetchScalarGridSpec(
            num_scalar_prefetch=1,
            grid=(n_seq, n_steps),
            in_specs=[seq_block((n_rows, IDX_DIM)), seq_block((n_rows, 1)), seq_block((IDX_DIM, LANES))]
                     + [page_block((IDX_DIM, PAGE_SIZE), pp) for pp in range(P)],
            out_specs=[pl.BlockSpec((None, n_new, P * PAGE_SIZE), lambda b, c, pt: (b, 0, c)),
                       seq_block((n_new, LANES))],
        ),
        out_shape=[jax.ShapeDtypeStruct((n_seq, n_new, n_past), f32),
                   jax.ShapeDtypeStruct((n_seq, n_new, LANES), f32)],
        compiler_params=pltpu.CompilerParams(
            dimension_semantics=("arbitrary", "arbitrary"), vmem_limit_bytes=VMEM_LIMIT),
        name="sample_index",
    )(page_table, qh, wc, kin, *([cki] * P))

    out = pl.pallas_call(
        functools.partial(_sample_attn_kernel, pages_per_step=P, n_new=n_new, n_past=n_past, topk=topk),
        grid_spec=pltpu.PrefetchScalarGridSpec(
            num_scalar_prefetch=1,
            grid=(n_seq, n_steps),
            in_specs=[_smem_spec(), seq_block((n_new, n_past)), seq_block((n_new, LANES)),
                      seq_block((n_rows, HEAD_DIM)), seq_block((LANES, KV_DIM)), seq_block((LANES, KV_DIM))]
                     + [page_block((page_cols, HEAD_DIM), pp) for pp in range(P)]
                     + [page_block((page_cols, HEAD_DIM), pp) for pp in range(P)],
            out_specs=seq_block((n_rows, HEAD_DIM)),
            scratch_shapes=[
                pltpu.VMEM((n_new, n_past + LANES), f32),
                pltpu.VMEM((N_KV_HEADS, P * PAGE_SIZE, HEAD_DIM), bf16),
                pltpu.VMEM((N_KV_HEADS, P * PAGE_SIZE, HEAD_DIM), bf16),
                pltpu.VMEM((n_new, 1), f32),
                pltpu.VMEM((n_rows, 1), f32),
                pltpu.VMEM((n_rows, 1), f32),
                pltpu.VMEM((n_rows, HEAD_DIM), f32),
            ],
        ),
        out_shape=jax.ShapeDtypeStruct((n_seq, n_rows, HEAD_DIM), f32),
        compiler_params=pltpu.CompilerParams(
            dimension_semantics=("arbitrary", "arbitrary"), vmem_limit_bytes=VMEM_LIMIT),
        name="sample_attn",
    )(page_table, rel_bias, past_keys, new_keys, qs, kn, vn, *([ck] * P), *([cv] * P))

    out = out.reshape(n_seq, N_HEADS, n_new, HEAD_DIM).transpose(0, 2, 1, 3)
    return out.reshape(n_seq * n_new, N_HEADS * HEAD_DIM).astype(bf16)


def _rglru_kernel(u_ref, buf_ref, h0_ref, cw_ref, cb_ref, wg_ref, bg_ref, lam_ref,
                  y_ref, nbuf_ref, hT_ref,
                  ext_ref, a_ref, b_ref, hs_ref, h_ref, *, tt):
    t = pl.program_id(1)
    head = SUBLANES

    @pl.when(t == 0)
    def _():
        ext_ref[head - (CONV_W - 1):head, :] = buf_ref[...]
        h_ref[...] = h0_ref[...]

    ext_ref[head:head + tt, :] = u_ref[...]
    xc = cb_ref[...] + ext_ref[head - 3:head - 3 + tt, :] * cw_ref[0:1, :]
    for j in range(1, CONV_W):
        xc = xc + ext_ref[head - 3 + j:head - 3 + j + tt, :] * cw_ref[j:j + 1, :]
    tail = ext_ref[head + tt - (CONV_W - 1):head + tt, :]
    nbuf_ref[...] = tail
    ext_ref[head - (CONV_W - 1):head, :] = tail

    lam = lam_ref[...]
    neg = -lam
    softplus = jnp.maximum(neg, 0.0) + jnp.log1p(jnp.exp(-jnp.abs(neg)))
    for n in range(LRU_BLOCKS):
        ns = slice(n * LRU_BLOCK_W, (n + 1) * LRU_BLOCK_W)
        xn = xc[:, ns]
        gates = _dot(xn.astype(bf16), wg_ref[n]) + bg_ref[n]
        r = jax.nn.sigmoid(gates[:, :LRU_BLOCK_W])
        ig = jax.nn.sigmoid(gates[:, LRU_BLOCK_W:])
        log_a = (-LRU_C) * r * softplus[:, ns]
        a = jnp.exp(log_a)
        a_ref[:, ns] = a
        b_ref[:, ns] = jnp.sqrt(-jnp.tanh(log_a) * (1.0 + a * a)) * (ig * xn)

    def step(r_, h):
        h = a_ref[pl.ds(r_, 1), :] * h + b_ref[pl.ds(r_, 1), :]
        hs_ref[pl.ds(r_, 1), :] = h
        return h

    h = lax.fori_loop(0, tt, step, h_ref[...], unroll=8)
    h_ref[...] = h
    hT_ref[...] = h
    y_ref[...] = hs_ref[...].astype(y_ref.dtype)


def _rglru(u, conv_buf, h0, conv_w, conv_b, w_rg, b_rg, w_ig, b_ig, lam):
    nb, seq, _ = u.shape
    tt = min(256, seq)
    assert seq % tt == 0 and tt % SUBLANES == 0 and tt >= CONV_W - 1
    wg = jnp.concatenate([w_rg, w_ig], axis=-1).astype(bf16)
    bg = jnp.concatenate([b_rg, b_ig], axis=-1).reshape(LRU_BLOCKS, 1, 2 * LRU_BLOCK_W)

    def per_seq(rows):
        return pl.BlockSpec((None, rows, D_RNN), lambda b, t: (b, 0, 0))

    y, nbuf, hT = pl.pallas_call(
        functools.partial(_rglru_kernel, tt=tt),
        grid=(nb, seq // tt),
        in_specs=[pl.BlockSpec((None, tt, D_RNN), lambda b, t: (b, t, 0)),
                  per_seq(CONV_W - 1), per_seq(1),
                  _const_spec((CONV_W, D_RNN)), _const_spec((1, D_RNN)),
                  _const_spec(wg.shape), _const_spec(bg.shape), _const_spec((1, D_RNN))],
        out_specs=[pl.BlockSpec((None, tt, D_RNN), lambda b, t: (b, t, 0)),
                   per_seq(CONV_W - 1), per_seq(1)],
        out_shape=[jax.ShapeDtypeStruct((nb, seq, D_RNN), bf16),
                   jax.ShapeDtypeStruct((nb, CONV_W - 1, D_RNN), f32),
                   jax.ShapeDtypeStruct((nb, 1, D_RNN), f32)],
        scratch_shapes=[
            pltpu.VMEM((SUBLANES + tt, D_RNN), f32),
            pltpu.VMEM((tt, D_RNN), f32),
            pltpu.VMEM((tt, D_RNN), f32),
            pltpu.VMEM((tt, D_RNN), f32),
            pltpu.VMEM((1, D_RNN), f32),
        ],
        compiler_params=pltpu.CompilerParams(
            dimension_semantics=("arbitrary", "arbitrary"), vmem_limit_bytes=VMEM_LIMIT),
        name="rglru",
    )(u, conv_buf, h0.reshape(nb, 1, D_RNN), conv_w, conv_b.reshape(1, D_RNN), wg, bg,
      lam.reshape(1, D_RNN))
    return y, nbuf, hT.reshape(nb, D_RNN)


def _merge_ffn_kernel(x_ref, attn_ref, lru_ref, ga_ref, gb_ref, woa_ref, wol_ref, wout_ref,
                      gf_ref, wfg_ref, wfu_ref, wfd_ref, gfin_ref, y_ref):
    merged = (jax.nn.sigmoid(ga_ref[...]) * _dot(attn_ref[...], woa_ref[...])
              + jax.nn.sigmoid(gb_ref[...]) * _dot(lru_ref[...], wol_ref[...]))
    h = x_ref[...] + _dot(merged.astype(bf16), wout_ref[...])
    hn = (h * lax.rsqrt(jnp.mean(h * h, axis=-1, keepdims=True) + EPS)) * gf_ref[...]
    hn = hn.astype(bf16)
    act = jax.nn.silu(_dot(hn, wfg_ref[...])) * _dot(hn, wfu_ref[...])
    y = h + _dot(act.astype(bf16), wfd_ref[...])
    y_ref[...] = (y * lax.rsqrt(jnp.mean(y * y, axis=-1, keepdims=True) + EPS)) * gfin_ref[...]


def _merge_ffn(x2, attn, lru, ga, gb, weights, g_ffn, g_final):
    n = x2.shape[0]
    tm = min(256, n)
    assert n % tm == 0
    woa, wol, wout, wfg, wfu, wfd = weights

    def rows(width):
        return pl.BlockSpec((tm, width), lambda i: (i, 0))

    return pl.pallas_call(
        _merge_ffn_kernel,
        grid=(n // tm,),
        in_specs=[rows(D_MODEL), rows(N_HEADS * HEAD_DIM), rows(D_RNN), rows(D_MODEL), rows(D_MODEL),
                  _const_spec(woa.shape), _const_spec(wol.shape), _const_spec(wout.shape),
                  _const_spec((1, D_MODEL)), _const_spec(wfg.shape), _const_spec(wfu.shape),
                  _const_spec(wfd.shape), _const_spec((1, D_MODEL))],
        out_specs=rows(D_MODEL),
        out_shape=jax.ShapeDtypeStruct((n, D_MODEL), f32),
        compiler_params=pltpu.CompilerParams(
            dimension_semantics=("arbitrary",), vmem_limit_bytes=VMEM_LIMIT),
        name="merge_ffn",
    )(x2, attn, lru, ga, gb, woa, wol, wout, g_ffn.reshape(1, D_MODEL), wfg, wfu, wfd,
      g_final.reshape(1, D_MODEL))


def kernel(x_prompt, x_sample, cache_k, cache_v, cache_kidx, state_conv, state_rnn, page_table,
           rel_bias, g_mix, w_in, conv_w, conv_b, w_rgate, b_rgate, w_igate, b_igate, lru_lambda,
           w_o_attn, w_o_lru, w_out, g_ffn, w_ffn_gate, w_ffn_up, w_ffn_down, g_final):
    assert w_in.shape[0] == 1, "one trunk layer"
    batch, seq, _ = x_prompt.shape
    n_seq, n_new, _ = x_sample.shape
    layer = 0

    w_parts = _split_w_in(w_in[layer])
    lru_w = (conv_w[layer], conv_b[layer], w_rgate[layer], b_rgate[layer], w_igate[layer],
             b_igate[layer], lru_lambda[layer])
    out_w = tuple(w[layer].astype(bf16)
                  for w in (w_o_attn, w_o_lru, w_out, w_ffn_gate, w_ffn_up, w_ffn_down))

    xp = x_prompt.reshape(batch * seq, D_MODEL)
    q, k, v, kb, vb, qi, kie, kio, kw, u, ga, gb = _proj(xp, g_mix[layer], w_parts)
    attn = _prompt_attn(rel_bias, q, qi, kw, kb, vb, kie, kio, batch, seq)
    lru, buf_p, h_p = _rglru(u.reshape(batch, seq, D_RNN),
                             jnp.zeros((batch, CONV_W - 1, D_RNN), f32),
                             jnp.zeros((batch, D_RNN), f32), *lru_w)
    y_prompt = _merge_ffn(xp, attn, lru.reshape(batch * seq, D_RNN), ga, gb, out_w,
                          g_ffn[layer], g_final).reshape(batch, seq, D_MODEL)
    new_k_prompt = k.reshape(1, batch, seq, N_KV_HEADS, HEAD_DIM)
    new_v_prompt = v.reshape(1, batch, seq, N_KV_HEADS, HEAD_DIM)
    new_kidx_prompt = kw[:, :IDX_DIM].reshape(1, batch, seq, IDX_DIM)

    xs = x_sample.reshape(n_seq * n_new, D_MODEL)
    q, k, v, kb, vb, qi, kie, kio, kw, u, ga, gb = _proj(xs, g_mix[layer], w_parts)
    attn = _sample_attention(page_table, rel_bias, cache_k[layer], cache_v[layer], cache_kidx[layer],
                             q, qi, kw, kb, vb, n_seq, n_new)
    lru, buf_s, h_s = _rglru(u.reshape(n_seq, n_new, D_RNN), state_conv[layer], state_rnn[layer], *lru_w)
    y_sample = _merge_ffn(xs, attn, lru.reshape(n_seq * n_new, D_RNN), ga, gb, out_w,
                          g_ffn[layer], g_final).reshape(n_seq, n_new, D_MODEL)
    new_k_sample = k.reshape(1, n_seq, n_new, N_KV_HEADS, HEAD_DIM)
    new_v_sample = v.reshape(1, n_seq, n_new, N_KV_HEADS, HEAD_DIM)
    new_kidx_sample = kw[:, :IDX_DIM].reshape(1, n_seq, n_new, IDX_DIM)

    return (y_prompt, y_sample, new_k_prompt, new_v_prompt, new_kidx_prompt, buf_p[None], h_p[None],
            new_k_sample, new_v_sample, new_kidx_sample, buf_s[None], h_s[None])
```

```python
import functools
import math

import jax
import jax.numpy as jnp
from jax import lax
from jax.experimental import pallas as pl
from jax.experimental.pallas import tpu as pltpu

D_MODEL = 1024
N_HEADS = 8
N_KV_HEADS = 2
HEAD_DIM = 128
GROUP = N_HEADS // N_KV_HEADS
IDX_HEADS = 8
IDX_DIM = 64
TOPK_MAX = 256
D_RNN = D_MODEL
LRU_BLOCKS = 8
LRU_BLOCK_W = D_RNN // LRU_BLOCKS
CONV_W = 4
LRU_C = 8.0
N_BUCKETS = 32
MAX_EXACT = N_BUCKETS // 2
MAX_DISTANCE = 128
EPS = 1e-6
NEG_INF = -1e30
PAGE_SIZE = 128
KV_DIM = N_KV_HEADS * HEAD_DIM
LOG2E = math.log2(math.e)
Q_SCALE = HEAD_DIM ** -0.5 * LOG2E
WI_SCALE = IDX_HEADS ** -0.5 * IDX_DIM ** -0.5

LANES = 128
SUBLANES = 8
BF16_SUBLANES = 16
VMEM_LIMIT = 56 * 1024 * 1024

_BUCKET_STEPS = tuple(
    math.ceil(MAX_EXACT * (MAX_DISTANCE / MAX_EXACT) ** (k / (N_BUCKETS - MAX_EXACT)))
    for k in range(1, N_BUCKETS - MAX_EXACT))
M_INIT = -1e29
BOUND_SLACK = 1.0 + 2.0 ** -6
MIN_DENOMINATOR = 2.0 ** -60

_INT_MIN = -2 ** 31
_F32_MANTISSA_MASK = 0x007FFFFF
_INF_KEY = 0x7F800000 - _F32_MANTISSA_MASK
_BF16_MANTISSA_MASK = 0x7F
_INF_KEY16 = 0x7F80 - _BF16_MANTISSA_MASK
bf16 = jnp.bfloat16
f32 = jnp.float32
i32 = jnp.int32


def _dot(a, b):
    return jnp.dot(a, b, preferred_element_type=f32)


def _dot_nt(a, b):
    return lax.dot_general(a, b, (((1,), (1,)), ((), ())), preferred_element_type=f32)


def _const_spec(shape):
    zeros = (0,) * len(shape)
    return pl.BlockSpec(shape, lambda *_: zeros, pipeline_mode=pl.Buffered(1))


def _smem_spec():
    return pl.BlockSpec(memory_space=pltpu.SMEM)


def _key_to_float(key):
    mag = jnp.minimum(jnp.abs(jnp.maximum(key, -_INF_KEY)), _INF_KEY)
    bits = jnp.where(mag == 0, 0, mag + _F32_MANTISSA_MASK)
    return lax.bitcast_convert_type(jnp.where(key < 0, bits | _INT_MIN, bits), f32)


def _grid_to_float(key):
    mag = jnp.minimum(jnp.abs(jnp.maximum(key, -_INF_KEY16)), _INF_KEY16)
    bits = lax.shift_left(jnp.where(mag == 0, 0, mag + _BF16_MANTISSA_MASK), 16)
    return lax.bitcast_convert_type(jnp.where(key < 0, bits | _INT_MIN, bits), f32)


def _bucket(dist):
    d = jnp.maximum(dist, 0)
    large = jnp.full(d.shape, MAX_EXACT, i32)
    for step in _BUCKET_STEPS:
        large = large + (d >= step).astype(i32)
    return jnp.where(d < MAX_EXACT, d, large)


def _proj_kernel(x_ref, g_ref, wq_ref, wkv_ref, wqi_ref, wke_ref, wko_ref, wkw_ref, wu_ref,
                 wga_ref, wgb_ref,
                 q_ref, k_ref, v_ref, kb_ref, vb_ref, qi_ref, kie_ref, kio_ref, kw_ref,
                 u_ref, ga_ref, gb_ref):
    x = x_ref[...]
    ms = jnp.mean(x * x, axis=-1, keepdims=True)
    xn = ((x * lax.rsqrt(ms + EPS)) * g_ref[...]).astype(bf16)

    q_ref[...] = (_dot(xn, wq_ref[...]) * Q_SCALE).astype(bf16)
    kv = _dot(xn, wkv_ref[...])
    tm = x.shape[0]
    for n in range(N_KV_HEADS):
        rows = pl.ds(n, tm, stride=N_KV_HEADS)
        k_ref[rows, :] = kv[:, n * HEAD_DIM:(n + 1) * HEAD_DIM]
        v_ref[rows, :] = kv[:, KV_DIM + n * HEAD_DIM:KV_DIM + (n + 1) * HEAD_DIM]
    kb_ref[...] = kv[:, :KV_DIM].astype(bf16)
    vb_ref[...] = kv[:, KV_DIM:].astype(bf16)
    qi_ref[...] = _dot(xn, wqi_ref[...]).astype(bf16)
    kie_ref[...] = _dot(xn, wke_ref[...]).astype(bf16)
    kio_ref[...] = _dot(xn, wko_ref[...]).astype(bf16)
    kw = _dot(xn, wkw_ref[...])
    lane = lax.broadcasted_iota(i32, kw.shape, 1)
    kw_ref[...] = jnp.where(lane >= IDX_DIM, kw * WI_SCALE, kw)
    u_ref[...] = _dot(xn, wu_ref[...])
    ga_ref[...] = _dot(xn, wga_ref[...])
    gb_ref[...] = _dot(xn, wgb_ref[...])


def _split_w_in(w_in):
    sizes = (N_HEADS * HEAD_DIM, KV_DIM, KV_DIM, IDX_HEADS * IDX_DIM, IDX_DIM, IDX_HEADS,
             D_RNN, D_MODEL, D_MODEL)
    parts, acc = [], 0
    for s in sizes:
        parts.append(w_in[:, acc:acc + s])
        acc += s
    wq, wk, wv, wqi, wki, wwi, wu, wga, wgb = parts
    zk = jnp.zeros_like(wki)
    wkv = jnp.concatenate([wk, wv], axis=1)
    wke = jnp.concatenate([wki, zk], axis=1)
    wko = jnp.concatenate([zk, wki], axis=1)
    wkw = jnp.concatenate(
        [wki, wwi, jnp.zeros((w_in.shape[0], LANES - IDX_DIM - IDX_HEADS), w_in.dtype)], axis=1)
    return tuple(w.astype(bf16) for w in (wq, wkv, wqi, wke, wko, wkw, wu, wga, wgb))


def _proj(x2, g_mix, w_parts):
    n = x2.shape[0]
    tm = min(512, n)
    assert n % tm == 0

    def rows(width):
        return pl.BlockSpec((tm, width), lambda i: (i, 0))

    outs = ((1, N_HEADS * HEAD_DIM, bf16), (N_KV_HEADS, HEAD_DIM, f32), (N_KV_HEADS, HEAD_DIM, f32),
            (1, KV_DIM, bf16), (1, KV_DIM, bf16), (1, IDX_HEADS * IDX_DIM, bf16), (1, LANES, bf16),
            (1, LANES, bf16), (1, LANES, f32), (1, D_RNN, f32), (1, D_MODEL, f32), (1, D_MODEL, f32))
    return pl.pallas_call(
        _proj_kernel,
        grid=(n // tm,),
        in_specs=[rows(D_MODEL), _const_spec((1, D_MODEL))] + [_const_spec(w.shape) for w in w_parts],
        out_specs=[pl.BlockSpec((r * tm, w), lambda i: (i, 0)) for r, w, _ in outs],
        out_shape=[jax.ShapeDtypeStruct((r * n, w), d) for r, w, d in outs],
        compiler_params=pltpu.CompilerParams(
            dimension_semantics=("arbitrary",), vmem_limit_bytes=VMEM_LIMIT),
        name="proj",
    )(x2, g_mix.reshape(1, D_MODEL), *w_parts)


def _greedy_bits(count_at, n_bits, n_start, k):
    def body(it, carry):
        x, n_at = carry
        cand = x | lax.shift_left(jnp.int32(1), jnp.int32(n_bits - 1) - it)
        n = count_at(cand)
        ok = n >= k
        return jnp.where(ok, cand, x), jnp.where(ok, n, n_at)

    return lax.fori_loop(0, n_bits, body, (jnp.zeros(n_start.shape, i32), n_start))


def _kth_largest(count_ge, n_all, k):
    t_u, n_at = _greedy_bits(lambda u: count_ge(_key_to_float(u ^ _INT_MIN)), 32, n_all, k)
    return t_u ^ _INT_MIN, n_at


def _tie_cut(count_eq_below, shape, need, n_bits):
    def body(it, x):
        bit = lax.shift_left(jnp.int32(1), jnp.int32(n_bits - 1) - it)
        cand = x | bit
        cnt = count_eq_below(cand)
        return jnp.where(cnt < need, cand, x)

    return lax.fori_loop(0, n_bits, body, jnp.zeros(shape, i32))


def _prompt_attn_kernel(rb_ref, q_ref, qi_ref, kw_ref, kb_ref, vb_ref, kie_ref, kio_ref, o_ref,
                        score_ref, floor_ref, mask_ref, bias_ref, mx_ref, acc_ref, knorm_ref,
                        *, tq, topk, seq):
    b = pl.program_id(0)
    i = pl.program_id(1)
    tk = tq
    grid = jnp.bfloat16
    krow = lax.broadcasted_iota(i32, (tk, tq), 0)
    qcol = lax.broadcasted_iota(i32, (tk, tq), 1)

    @pl.when((b == 0) & (i == 0))
    def _():
        for t in range(2):
            bucket = _bucket(t * tq + krow - qcol)
            for h in range(N_HEADS):
                bias_ref[t * N_HEADS + h] = jnp.zeros((tq, tk), f32)

            def fill(bb, carry, t=t, bucket=bucket):
                hit = bucket == bb
                for h in range(N_HEADS):
                    bias_ref[t * N_HEADS + h] = jnp.where(
                        hit, rb_ref[bb, h] * LOG2E, bias_ref[t * N_HEADS + h])
                return carry

            lax.fori_loop(0, N_BUCKETS, fill, 0)

    w_t = kw_ref[...].T

    def score_block(j, causal):
        off = pl.multiple_of(j * tk, tk)
        ke = kie_ref[pl.ds(off, tk), :]
        ko = kio_ref[pl.ds(off, tk), :]
        s = jnp.zeros((tk, tq), f32)
        for p in range(IDX_HEADS // 2):
            rhs = qi_ref[:, p * LANES:(p + 1) * LANES]
            c0 = IDX_DIM + 2 * p
            s = s + jnp.maximum(_dot_nt(ke, rhs), 0.0) * w_t[c0:c0 + 1, :]
            s = s + jnp.maximum(_dot_nt(ko, rhs), 0.0) * w_t[c0 + 1:c0 + 2, :]
        if causal:
            s = jnp.where(krow <= qcol, s, NEG_INF)
        score_ref[pl.ds(off, tk), :] = s
        near = s.astype(grid)
        bits = pltpu.bitcast(near, jnp.int16)
        down = pltpu.bitcast(jnp.where(near > 0, bits - 1, bits + 1), grid)
        floor_ref[pl.ds(off, tk), :] = jnp.where(near.astype(f32) > s, down, near)

    def two_per_trip(n, fn):
        n = jnp.maximum(n, 0)

        def pair(p, carry):
            fn(2 * p)
            fn(2 * p + 1)
            return carry

        lax.fori_loop(0, n // 2, pair, 0)

        @pl.when(n % 2 == 1)
        def _():
            fn(n - 1)

    two_per_trip(i, lambda j: score_block(j, False))
    score_block(i, True)

    def floor_count(t):
        t = t.astype(grid)

        def body(j, c):
            off = pl.multiple_of(j * tk, tk)
            hit = jnp.where(floor_ref[pl.ds(off, tk), :] >= t, jnp.ones((), jnp.int16), jnp.zeros((), jnp.int16))
            rows = [hit[g * BF16_SUBLANES:(g + 1) * BF16_SUBLANES] for g in range(tk // BF16_SUBLANES)]
            while len(rows) > 1:
                rows = [a + b for a, b in zip(rows[0::2], rows[1::2])]
            return c + rows[0]

        c = lax.fori_loop(0, i + 1, body, jnp.zeros((BF16_SUBLANES, tq), jnp.int16))
        return jnp.sum(c.astype(i32), axis=0, keepdims=True)


    def score_count(pred):
        def body(j, c):
            off = pl.multiple_of(j * tk, tk)
            hit = pred(score_ref[pl.ds(off, tk), :], off).astype(i32)
            rows = [hit[g * SUBLANES:(g + 1) * SUBLANES] for g in range(tk // SUBLANES)]
            while len(rows) > 1:
                rows = [a + b for a, b in zip(rows[0::2], rows[1::2])]
            return c + rows[0]

        c = lax.fori_loop(0, i + 1, body, jnp.zeros((SUBLANES, tq), i32))
        return jnp.sum(c, axis=0, keepdims=True)

    def count_ge(t):
        return score_count(lambda sc, off: sc >= t)

    n_all = jnp.full((1, tq), (i + 1) * tk, i32)
    lo_key, n_lo = _greedy_bits(lambda key: floor_count(_grid_to_float(key - 2 ** 15)), 16, n_all, topk)
    lo = _grid_to_float(lo_key - 2 ** 15)
    hi = _grid_to_float(lo_key - 2 ** 15 + 1)
    width = hi - lo
    step = jnp.where(width < jnp.inf, width, 0.0) * 2.0 ** -16
    pos, n_ge = _greedy_bits(lambda j: count_ge(lo + j.astype(f32) * step), 16, n_lo, topk)
    thr = lo + pos.astype(f32) * step

    @pl.when(jnp.max(n_ge) > topk)
    def _():
        need = topk - score_count(lambda sc, off: sc > thr)
        cut = _tie_cut(lambda x: score_count(lambda sc, off: (sc == thr) & (krow + off < x)),
                       (1, tq), need, (seq - 1).bit_length())

        def demote(j, carry):
            off = pl.multiple_of(j * tk, tk)
            sc = score_ref[pl.ds(off, tk), :]
            score_ref[pl.ds(off, tk), :] = jnp.where((sc == thr) & (krow + off > cut), -jnp.inf, sc)
            return carry

        lax.fori_loop(0, i + 1, demote, 0)

    far_bias = [rb_ref[N_BUCKETS - 1, h] * LOG2E for h in range(N_HEADS)]

    def to_mask(j, causal):
        off = pl.multiple_of(j * tk, tk)
        sel = score_ref[pl.ds(off, tk), :] >= thr
        if causal:
            sel = sel & (krow <= qcol)
        mask_ref[:, pl.ds(off, tk)] = jnp.where(sel, 0.0, NEG_INF).T

    def mask_loop(j, carry):
        to_mask(j, False)
        return carry

    lax.fori_loop(0, i, mask_loop, 0)
    to_mask(i, True)

    def group_logits(j, mode):
        off = pl.multiple_of(j * tk, tk)
        mask = mask_ref[:, pl.ds(off, tk)]
        for n in range(N_KV_HEADS):
            ns = slice(n * HEAD_DIM, (n + 1) * HEAD_DIM)
            heads = range(n * GROUP, (n + 1) * GROUP)
            qs = jnp.concatenate([q_ref[:, h * HEAD_DIM:(h + 1) * HEAD_DIM] for h in heads], axis=0)
            s = _dot_nt(qs, kb_ref[pl.ds(off, tk), ns])
            per_head = []
            for g, h in enumerate(heads):
                sg = s[g * tq:(g + 1) * tq] + mask
                if mode != 0:
                    sg = sg + bias_ref[(2 - mode) * N_HEADS + h]
                per_head.append((h, sg))
            yield n, off, ns, per_head

    def over_blocks(block_fn):
        two_per_trip(i - 1, lambda j: block_fn(j, 0))

        @pl.when(i >= 1)
        def _():
            block_fn(i - 1, 1)

        block_fn(i, 2)

    def exact_max():
        mx_ref[...] = jnp.full(mx_ref.shape, NEG_INF, f32)

        def max_block(j, mode):
            for _, _, _, per_head in group_logits(j, mode):
                for h, sg in per_head:
                    part = sg[:, 0:LANES]
                    for c in range(1, tk // LANES):
                        part = jnp.maximum(part, sg[:, c * LANES:(c + 1) * LANES])
                    if mode == 0:
                        part = part + far_bias[h]
                    mx_ref[h] = jnp.maximum(mx_ref[h], part)

        over_blocks(max_block)
        for h in range(N_HEADS):
            mx_ref[h] = jnp.broadcast_to(jnp.max(mx_ref[h], axis=-1, keepdims=True), (tq, LANES))

    ones_cols = jnp.ones((tk, HEAD_DIM), bf16)

    def value_sweep():
        acc_ref[...] = jnp.zeros(acc_ref.shape, f32)

        def pv_block(j, mode):
            for n, off, ns, per_head in group_logits(j, mode):
                ps = []
                for h, sg in per_head:
                    shift = mx_ref[h] - far_bias[h] if mode == 0 else mx_ref[h]
                    shift = jnp.concatenate([shift] * (tk // LANES), axis=1)
                    ps.append(jnp.exp2(sg - shift).astype(bf16))
                values = jnp.concatenate([vb_ref[pl.ds(off, tk), ns], ones_cols], axis=1)
                acc_ref[n] = acc_ref[n] + _dot(jnp.concatenate(ps, axis=0), values)

        over_blocks(pv_block)

    @pl.when(i == 0)
    def _():
        knorm_ref[...] = jnp.zeros(knorm_ref.shape, f32)

    diag = pl.multiple_of(i * tk, tk)
    for n in range(N_KV_HEADS):
        kf = kb_ref[pl.ds(diag, tk), n * HEAD_DIM:(n + 1) * HEAD_DIM].astype(f32)
        ksq = jnp.max(jnp.sum(kf * kf, axis=-1, keepdims=True), axis=0, keepdims=True)
        knorm_ref[n] = jnp.maximum(knorm_ref[n], jnp.broadcast_to(ksq, knorm_ref.shape[1:]))
    for h in range(N_HEADS):
        qf = q_ref[:, h * HEAD_DIM:(h + 1) * HEAD_DIM].astype(f32)
        qsq = jnp.sum(qf * qf, axis=-1, keepdims=True)
        bias_max = lax.fori_loop(1, N_BUCKETS, lambda bb, m, h=h: jnp.maximum(m, rb_ref[bb, h]),
                                 rb_ref[0, h]) * LOG2E
        bound = jnp.sqrt(qsq * knorm_ref[h // GROUP][0:1, 0:1]) * BOUND_SLACK + bias_max
        mx_ref[h] = jnp.broadcast_to(bound, (tq, LANES))
    value_sweep()

    @pl.when(jnp.min(acc_ref[:, :, HEAD_DIM:]) < MIN_DENOMINATOR)
    def _():
        exact_max()
        value_sweep()

    for h in range(N_HEADS):
        acc = acc_ref[h // GROUP, (h % GROUP) * tq:(h % GROUP + 1) * tq, :]
        out = acc[:, :HEAD_DIM] / acc[:, HEAD_DIM:]
        o_ref[:, h * HEAD_DIM:(h + 1) * HEAD_DIM] = out.astype(o_ref.dtype)


def _prompt_attn(rel_bias, q, qi, kw, kb, vb, kie, kio, batch, seq):
    tq = min(256, seq)
    assert seq % tq == 0 and tq % LANES == 0 and tq >= MAX_DISTANCE
    nq = seq // tq
    topk = min(TOPK_MAX, seq // 4)

    def qrows(width):
        return pl.BlockSpec((tq, width), lambda b, i: (b * nq + i, 0))

    def krows(width):
        return pl.BlockSpec((seq, width), lambda b, i: (b, 0))

    kern = functools.partial(_prompt_attn_kernel, tq=tq, topk=topk, seq=seq)
    return pl.pallas_call(
        kern,
        grid=(batch, nq),
        in_specs=[_smem_spec(), qrows(N_HEADS * HEAD_DIM), qrows(IDX_HEADS * IDX_DIM), qrows(LANES),
                  krows(KV_DIM), krows(KV_DIM), krows(LANES), krows(LANES)],
        out_specs=qrows(N_HEADS * HEAD_DIM),
        out_shape=jax.ShapeDtypeStruct((batch * seq, N_HEADS * HEAD_DIM), bf16),
        scratch_shapes=[
            pltpu.VMEM((seq, tq), f32),
            pltpu.VMEM((seq, tq), jnp.bfloat16),
            pltpu.VMEM((tq, seq), f32),
            pltpu.VMEM((2 * N_HEADS, tq, tq), f32),
            pltpu.VMEM((N_HEADS, tq, LANES), f32),
            pltpu.VMEM((N_KV_HEADS, GROUP * tq, 2 * HEAD_DIM), f32),
            pltpu.VMEM((N_KV_HEADS, SUBLANES, LANES), f32),
        ],
        compiler_params=pltpu.CompilerParams(
            dimension_semantics=("arbitrary", "arbitrary"), vmem_limit_bytes=VMEM_LIMIT),
        name="prompt_attn",
    )(rel_bias, q, qi, kw, kb, vb, kie, kio)


def _fetch_pages(pt_ref, pairs, sem_ref, pages_per_step):
    n_steps = pl.num_programs(1)
    t = pl.program_id(0) * n_steps + pl.program_id(1)
    slot = t % 2

    def copies(step, slot_):
        seq, chunk = step // n_steps, step % n_steps
        for pp in range(pages_per_step):
            page = pt_ref[seq, chunk * pages_per_step + pp]
            for a, (hbm_ref, buf_ref) in enumerate(pairs):
                yield pltpu.make_async_copy(hbm_ref.at[page], buf_ref.at[slot_, pp], sem_ref.at[a, slot_, pp])

    @pl.when(t == 0)
    def _():
        for cp in copies(t, slot):
            cp.start()

    @pl.when(t + 1 < pl.num_programs(0) * n_steps)
    def _():
        for cp in copies(t + 1, 1 - slot):
            cp.start()

    for cp in copies(t, slot):
        cp.wait()
    return slot


def _sample_index_kernel(pt_ref, qh_ref, wc_ref, kin_ref, cki_hbm, past_ref, new_ref, buf_ref, sem_ref,
                         *, pages_per_step, n_new):
    slot = _fetch_pages(pt_ref, [(cki_hbm, buf_ref)], sem_ref, pages_per_step)
    c = pl.program_id(1)
    qh = qh_ref[...]
    wc = wc_ref[...]

    def scores(keys_t):
        s = jnp.maximum(_dot(qh, keys_t), 0.0) * wc
        out = s[0:n_new]
        for h in range(1, IDX_HEADS):
            out = out + s[h * n_new:(h + 1) * n_new]
        return out

    pages = jnp.concatenate([buf_ref[slot, pp].astype(bf16) for pp in range(pages_per_step)], axis=1)
    past_ref[...] = scores(pages)

    @pl.when(c == 0)
    def _():
        sc = scores(kin_ref[...])
        tok = lax.broadcasted_iota(i32, sc.shape, 0)
        key = lax.broadcasted_iota(i32, sc.shape, 1)
        new_ref[...] = jnp.where(key <= tok, sc, NEG_INF)


def _sample_attn_kernel(pt_ref, rb_ref, past_ref, new_ref, q_ref, kn_ref, vn_ref, ck_hbm, cv_hbm, o_ref,
                        score_ref, kcat_ref, vcat_ref, thr_ref, m_ref, l_ref, acc_ref,
                        kbuf_ref, vbuf_ref, sem_ref, *, pages_per_step, n_new, n_past, topk):
    P = pages_per_step
    slot = _fetch_pages(pt_ref, [(ck_hbm, kbuf_ref), (cv_hbm, vbuf_ref)], sem_ref, P)
    c = pl.program_id(1)
    n_steps = pl.num_programs(1)
    n_rows = N_HEADS * n_new
    half = GROUP * n_new
    step_keys = P * PAGE_SIZE
    n_keys = n_past + LANES

    def head_column(bucket_row):
        r = lax.broadcasted_iota(i32, (n_rows, 1), 0) // n_new
        out = jnp.zeros((n_rows, 1), f32)
        for h in range(N_HEADS):
            out = jnp.where(r == h, rb_ref[bucket_row, h] * LOG2E, out)
        return out

    def bias_tile(dist):
        bucket = _bucket(dist)

        def fill(bb, acc):
            return jnp.where(bucket == bb, head_column(bb), acc)

        return lax.fori_loop(0, N_BUCKETS, fill, jnp.zeros(dist.shape, f32))

    @pl.when(c == 0)
    def _():
        score_ref[:, :n_past] = past_ref[...]
        score_ref[:, n_past:] = new_ref[...]

        def count(pred):
            return jnp.sum(pred(score_ref[...]).astype(i32), axis=-1, keepdims=True)

        def count_ge(t):
            return count(lambda sc: sc >= t)

        thr_key, n_ge = _kth_largest(count_ge, jnp.full((n_new, 1), n_keys, i32), topk)
        thr = _key_to_float(thr_key)
        thr_ref[...] = thr

        @pl.when(jnp.max(n_ge) > topk)
        def _():
            need = topk - count_ge(_key_to_float(thr_key + 1))
            colk = lax.broadcasted_iota(i32, (n_new, n_keys), 1)
            cut = _tie_cut(lambda x: count(lambda sc: (sc == thr) & (colk < x)),
                           (n_new, 1), need, (n_keys - 1).bit_length())
            sc = score_ref[...]
            score_ref[...] = jnp.where((sc == thr) & (colk > cut), _key_to_float(thr_key - 1), sc)

        m_ref[...] = jnp.full(m_ref.shape, M_INIT, f32)
        l_ref[...] = jnp.zeros(l_ref.shape, f32)
        acc_ref[...] = jnp.zeros(acc_ref.shape, f32)

    thr = thr_ref[...]
    far_bias = head_column(N_BUCKETS - 1)
    tok = lax.broadcasted_iota(i32, (n_rows, LANES), 0) % n_new
    kpos = lax.broadcasted_iota(i32, (n_rows, LANES), 1)

    def online_update(n, s, values):
        rows = slice(n * half, (n + 1) * half)
        m_old = m_ref[rows]
        m_new = jnp.maximum(m_old, jnp.max(s, axis=-1, keepdims=True))
        alpha = jnp.exp2(m_old - m_new)
        p = jnp.exp2(s - m_new)
        l_ref[rows] = alpha * l_ref[rows] + jnp.sum(p, axis=-1, keepdims=True)
        acc_ref[rows] = alpha * acc_ref[rows] + _dot(p.astype(bf16), values)
        m_ref[rows] = m_new

    def tiled_mask(scores):
        mask = jnp.where(scores >= thr, 0.0, NEG_INF)
        return jnp.concatenate([mask] * GROUP, axis=0)

    for pp in range(P):
        for n in range(N_KV_HEADS):
            rows = pl.ds(n, PAGE_SIZE, stride=N_KV_HEADS)
            kcat_ref[n, pp * PAGE_SIZE:(pp + 1) * PAGE_SIZE, :] = kbuf_ref[slot, pp, rows, :].astype(bf16)
            vcat_ref[n, pp * PAGE_SIZE:(pp + 1) * PAGE_SIZE, :] = vbuf_ref[slot, pp, rows, :].astype(bf16)

    base = pl.multiple_of(c * step_keys, step_keys)
    mask = tiled_mask(score_ref[:, pl.ds(base, step_keys)])
    last_bias = lax.cond(c == n_steps - 1,
                         lambda: bias_tile(PAGE_SIZE + tok - kpos),
                         lambda: jnp.broadcast_to(far_bias, (n_rows, LANES)))
    for n in range(N_KV_HEADS):
        rows = slice(n * half, (n + 1) * half)
        s = _dot_nt(q_ref[rows, :], kcat_ref[n])
        s = jnp.concatenate([s[:, :step_keys - PAGE_SIZE] + far_bias[rows],
                             s[:, step_keys - PAGE_SIZE:] + last_bias[rows]], axis=1)
        online_update(n, s + mask, vcat_ref[n])

    @pl.when(c == n_steps - 1)
    def _():
        causal = (lax.broadcasted_iota(i32, (half, LANES), 1)
                  <= lax.broadcasted_iota(i32, (half, LANES), 0) % n_new)
        mask = jnp.where(causal, tiled_mask(score_ref[:, n_past:]), NEG_INF)
        bias = bias_tile(tok - kpos)
        for n in range(N_KV_HEADS):
            rows = slice(n * half, (n + 1) * half)
            ns = slice(n * HEAD_DIM, (n + 1) * HEAD_DIM)
            s = _dot_nt(q_ref[rows, :], kn_ref[:, ns]) + bias[rows]
            online_update(n, s + mask, vn_ref[:, ns])
        o_ref[...] = acc_ref[...] / l_ref[...]


def _sample_attention(page_table, rel_bias, cache_k, cache_v, cache_kidx, q, qi, kw, kb, vb, n_seq, n_new):
    n_pages = page_table.shape[1]
    n_pool = cache_k.shape[0]
    n_past = n_pages * PAGE_SIZE
    topk = min(TOPK_MAX, (n_past + n_new) // 4)
    P = min(32, n_pages)
    assert n_pages % P == 0 and n_new == SUBLANES
    n_steps = n_pages // P
    n_rows = N_HEADS * n_new
    page_cols = N_KV_HEADS * PAGE_SIZE

    qh = qi.reshape(n_seq, n_new, IDX_HEADS, IDX_DIM).transpose(0, 2, 1, 3).reshape(n_seq, n_rows, IDX_DIM)
    wc = kw[:, IDX_DIM:IDX_DIM + IDX_HEADS].reshape(n_seq, n_new, IDX_HEADS).transpose(0, 2, 1)
    wc = wc.reshape(n_seq, n_rows, 1)
    kin = kw[:, :IDX_DIM].astype(bf16).reshape(n_seq, n_new, IDX_DIM).transpose(0, 2, 1)
    kin = jnp.pad(kin, ((0, 0), (0, 0), (0, LANES - n_new)))
    qs = q.reshape(n_seq, n_new, N_HEADS, HEAD_DIM).transpose(0, 2, 1, 3).reshape(n_seq, n_rows, HEAD_DIM)
    pad = ((0, 0), (0, LANES - n_new), (0, 0))
    kn = jnp.pad(kb.reshape(n_seq, n_new, KV_DIM), pad)
    vn = jnp.pad(vb.reshape(n_seq, n_new, KV_DIM), pad)
    ck = cache_k.reshape(n_pool, page_cols, HEAD_DIM)
    cv = cache_v.reshape(n_pool, page_cols, HEAD_DIM)
    cki = jnp.swapaxes(cache_kidx, -1, -2)

    def seq_block(shape):
        nd = len(shape)
        return pl.BlockSpec((None,) + shape, lambda b, c, pt: (b,) + (0,) * nd)

    paged = pl.BlockSpec(memory_space=pl.ANY)

    past_keys, new_keys = pl.pallas_call(
        functools.partial(_sample_index_kernel, pages_per_step=P, n_new=n_new),
        grid_spec=pltpu.PrefetchScalarGridSpec(
            num_scalar_prefetch=1,
            grid=(n_seq, n_steps),
            in_specs=[seq_block((n_rows, IDX_DIM)), seq_block((n_rows, 1)), seq_block((IDX_DIM, LANES)),
                      paged],
            out_specs=[pl.BlockSpec((None, n_new, P * PAGE_SIZE), lambda b, c, pt: (b, 0, c)),
                       seq_block((n_new, LANES))],
            scratch_shapes=[pltpu.VMEM((2, P, IDX_DIM, PAGE_SIZE), f32),
                            pltpu.SemaphoreType.DMA((1, 2, P))],
        ),
        out_shape=[jax.ShapeDtypeStruct((n_seq, n_new, n_past), f32),
                   jax.ShapeDtypeStruct((n_seq, n_new, LANES), f32)],
        compiler_params=pltpu.CompilerParams(
            dimension_semantics=("arbitrary", "arbitrary"), vmem_limit_bytes=VMEM_LIMIT),
        name="sample_index",
    )(page_table, qh, wc, kin, cki)

    out = pl.pallas_call(
        functools.partial(_sample_attn_kernel, pages_per_step=P, n_new=n_new, n_past=n_past, topk=topk),
        grid_spec=pltpu.PrefetchScalarGridSpec(
            num_scalar_prefetch=1,
            grid=(n_seq, n_steps),
            in_specs=[_smem_spec(), seq_block((n_new, n_past)), seq_block((n_new, LANES)),
                      seq_block((n_rows, HEAD_DIM)), seq_block((LANES, KV_DIM)), seq_block((LANES, KV_DIM)),
                      paged, paged],
            out_specs=seq_block((n_rows, HEAD_DIM)),
            scratch_shapes=[
                pltpu.VMEM((n_new, n_past + LANES), f32),
                pltpu.VMEM((N_KV_HEADS, P * PAGE_SIZE, HEAD_DIM), bf16),
                pltpu.VMEM((N_KV_HEADS, P * PAGE_SIZE, HEAD_DIM), bf16),
                pltpu.VMEM((n_new, 1), f32),
                pltpu.VMEM((n_rows, 1), f32),
                pltpu.VMEM((n_rows, 1), f32),
                pltpu.VMEM((n_rows, HEAD_DIM), f32),
                pltpu.VMEM((2, P, page_cols, HEAD_DIM), f32),
                pltpu.VMEM((2, P, page_cols, HEAD_DIM), f32),
                pltpu.SemaphoreType.DMA((2, 2, P)),
            ],
        ),
        out_shape=jax.ShapeDtypeStruct((n_seq, n_rows, HEAD_DIM), f32),
        compiler_params=pltpu.CompilerParams(
            dimension_semantics=("arbitrary", "arbitrary"), vmem_limit_bytes=VMEM_LIMIT),
        name="sample_attn",
    )(page_table, rel_bias, past_keys, new_keys, qs, kn, vn, ck, cv)

    out = out.reshape(n_seq, N_HEADS, n_new, HEAD_DIM).transpose(0, 2, 1, 3)
    return out.reshape(n_seq * n_new, N_HEADS * HEAD_DIM).astype(bf16)


def _rglru_kernel(u_ref, buf_ref, h0_ref, cw_ref, cb_ref, wg_ref, bg_ref, lam_ref,
                  y_ref, nbuf_ref, hT_ref,
                  ext_ref, a_ref, b_ref, hs_ref, h_ref, *, tt):
    t = pl.program_id(1)
    head = SUBLANES

    @pl.when(t == 0)
    def _():
        ext_ref[head - (CONV_W - 1):head, :] = buf_ref[...]
        h_ref[...] = h0_ref[...]

    ext_ref[head:head + tt, :] = u_ref[...]
    xc = cb_ref[...] + ext_ref[head - 3:head - 3 + tt, :] * cw_ref[0:1, :]
    for j in range(1, CONV_W):
        xc = xc + ext_ref[head - 3 + j:head - 3 + j + tt, :] * cw_ref[j:j + 1, :]
    tail = ext_ref[head + tt - (CONV_W - 1):head + tt, :]
    nbuf_ref[...] = tail
    ext_ref[head - (CONV_W - 1):head, :] = tail

    lam = lam_ref[...]
    neg = -lam
    softplus = jnp.maximum(neg, 0.0) + jnp.log1p(jnp.exp(-jnp.abs(neg)))
    for n in range(LRU_BLOCKS):
        ns = slice(n * LRU_BLOCK_W, (n + 1) * LRU_BLOCK_W)
        xn = xc[:, ns]
        gates = _dot(xn.astype(bf16), wg_ref[n]) + bg_ref[n]
        r = jax.nn.sigmoid(gates[:, :LRU_BLOCK_W])
        ig = jax.nn.sigmoid(gates[:, LRU_BLOCK_W:])
        log_a = (-LRU_C) * r * softplus[:, ns]
        a = jnp.exp(log_a)
        a_ref[:, ns] = a
        b_ref[:, ns] = jnp.sqrt(-jnp.tanh(log_a) * (1.0 + a * a)) * (ig * xn)

    def step(r_, h):
        h = a_ref[pl.ds(r_, 1), :] * h + b_ref[pl.ds(r_, 1), :]
        hs_ref[pl.ds(r_, 1), :] = h
        return h

    h = lax.fori_loop(0, tt, step, h_ref[...], unroll=8)
    h_ref[...] = h
    hT_ref[...] = h
    y_ref[...] = hs_ref[...].astype(y_ref.dtype)


def _rglru(u, conv_buf, h0, conv_w, conv_b, w_rg, b_rg, w_ig, b_ig, lam):
    nb, seq, _ = u.shape
    tt = min(256, seq)
    assert seq % tt == 0 and tt % SUBLANES == 0 and tt >= CONV_W - 1
    wg = jnp.concatenate([w_rg, w_ig], axis=-1).astype(bf16)
    bg = jnp.concatenate([b_rg, b_ig], axis=-1).reshape(LRU_BLOCKS, 1, 2 * LRU_BLOCK_W)

    def per_seq(rows):
        return pl.BlockSpec((None, rows, D_RNN), lambda b, t: (b, 0, 0))

    y, nbuf, hT = pl.pallas_call(
        functools.partial(_rglru_kernel, tt=tt),
        grid=(nb, seq // tt),
        in_specs=[pl.BlockSpec((None, tt, D_RNN), lambda b, t: (b, t, 0)),
                  per_seq(CONV_W - 1), per_seq(1),
                  _const_spec((CONV_W, D_RNN)), _const_spec((1, D_RNN)),
                  _const_spec(wg.shape), _const_spec(bg.shape), _const_spec((1, D_RNN))],
        out_specs=[pl.BlockSpec((None, tt, D_RNN), lambda b, t: (b, t, 0)),
                   per_seq(CONV_W - 1), per_seq(1)],
        out_shape=[jax.ShapeDtypeStruct((nb, seq, D_RNN), bf16),
                   jax.ShapeDtypeStruct((nb, CONV_W - 1, D_RNN), f32),
                   jax.ShapeDtypeStruct((nb, 1, D_RNN), f32)],
        scratch_shapes=[
            pltpu.VMEM((SUBLANES + tt, D_RNN), f32),
            pltpu.VMEM((tt, D_RNN), f32),
            pltpu.VMEM((tt, D_RNN), f32),
            pltpu.VMEM((tt, D_RNN), f32),
            pltpu.VMEM((1, D_RNN), f32),
        ],
        compiler_params=pltpu.CompilerParams(
            dimension_semantics=("arbitrary", "arbitrary"), vmem_limit_bytes=VMEM_LIMIT),
        name="rglru",
    )(u, conv_buf, h0.reshape(nb, 1, D_RNN), conv_w, conv_b.reshape(1, D_RNN), wg, bg,
      lam.reshape(1, D_RNN))
    return y, nbuf, hT.reshape(nb, D_RNN)


def _merge_ffn_kernel(x_ref, attn_ref, lru_ref, ga_ref, gb_ref, woa_ref, wol_ref, wout_ref,
                      gf_ref, wfg_ref, wfu_ref, wfd_ref, gfin_ref, y_ref):
    merged = (jax.nn.sigmoid(ga_ref[...]) * _dot(attn_ref[...], woa_ref[...])
              + jax.nn.sigmoid(gb_ref[...]) * _dot(lru_ref[...], wol_ref[...]))
    h = x_ref[...] + _dot(merged.astype(bf16), wout_ref[...])
    hn = (h * lax.rsqrt(jnp.mean(h * h, axis=-1, keepdims=True) + EPS)) * gf_ref[...]
    hn = hn.astype(bf16)
    act = jax.nn.silu(_dot(hn, wfg_ref[...])) * _dot(hn, wfu_ref[...])
    y = h + _dot(act.astype(bf16), wfd_ref[...])
    y_ref[...] = (y * lax.rsqrt(jnp.mean(y * y, axis=-1, keepdims=True) + EPS)) * gfin_ref[...]


def _merge_ffn(x2, attn, lru, ga, gb, weights, g_ffn, g_final):
    n = x2.shape[0]
    tm = min(256, n)
    assert n % tm == 0
    woa, wol, wout, wfg, wfu, wfd = weights

    def rows(width):
        return pl.BlockSpec((tm, width), lambda i: (i, 0))

    return pl.pallas_call(
        _merge_ffn_kernel,
        grid=(n // tm,),
        in_specs=[rows(D_MODEL), rows(N_HEADS * HEAD_DIM), rows(D_RNN), rows(D_MODEL), rows(D_MODEL),
                  _const_spec(woa.shape), _const_spec(wol.shape), _const_spec(wout.shape),
                  _const_spec((1, D_MODEL)), _const_spec(wfg.shape), _const_spec(wfu.shape),
                  _const_spec(wfd.shape), _const_spec((1, D_MODEL))],
        out_specs=rows(D_MODEL),
        out_shape=jax.ShapeDtypeStruct((n, D_MODEL), f32),
        compiler_params=pltpu.CompilerParams(
            dimension_semantics=("arbitrary",), vmem_limit_bytes=VMEM_LIMIT),
        name="merge_ffn",
    )(x2, attn, lru, ga, gb, woa, wol, wout, g_ffn.reshape(1, D_MODEL), wfg, wfu, wfd,
      g_final.reshape(1, D_MODEL))


def kernel(x_prompt, x_sample, cache_k, cache_v, cache_kidx, state_conv, state_rnn, page_table,
           rel_bias, g_mix, w_in, conv_w, conv_b, w_rgate, b_rgate, w_igate, b_igate, lru_lambda,
           w_o_attn, w_o_lru, w_out, g_ffn, w_ffn_gate, w_ffn_up, w_ffn_down, g_final):
    assert w_in.shape[0] == 1, "one trunk layer"
    batch, seq, _ = x_prompt.shape
    n_seq, n_new, _ = x_sample.shape
    layer = 0

    w_parts = _split_w_in(w_in[layer])
    lru_w = (conv_w[layer], conv_b[layer], w_rgate[layer], b_rgate[layer], w_igate[layer],
             b_igate[layer], lru_lambda[layer])
    out_w = tuple(w[layer].astype(bf16)
                  for w in (w_o_attn, w_o_lru, w_out, w_ffn_gate, w_ffn_up, w_ffn_down))

    xp = x_prompt.reshape(batch * seq, D_MODEL)
    q, k, v, kb, vb, qi, kie, kio, kw, u, ga, gb = _proj(xp, g_mix[layer], w_parts)
    attn = _prompt_attn(rel_bias, q, qi, kw, kb, vb, kie, kio, batch, seq)
    lru, buf_p, h_p = _rglru(u.reshape(batch, seq, D_RNN),
                             jnp.zeros((batch, CONV_W - 1, D_RNN), f32),
                             jnp.zeros((batch, D_RNN), f32), *lru_w)
    y_prompt = _merge_ffn(xp, attn, lru.reshape(batch * seq, D_RNN), ga, gb, out_w,
                          g_ffn[layer], g_final).reshape(batch, seq, D_MODEL)
    new_k_prompt = k.reshape(1, batch, seq, N_KV_HEADS, HEAD_DIM)
    new_v_prompt = v.reshape(1, batch, seq, N_KV_HEADS, HEAD_DIM)
    new_kidx_prompt = kw[:, :IDX_DIM].reshape(1, batch, seq, IDX_DIM)

    xs = x_sample.reshape(n_seq * n_new, D_MODEL)
    q, k, v, kb, vb, qi, kie, kio, kw, u, ga, gb = _proj(xs, g_mix[layer], w_parts)
    attn = _sample_attention(page_table, rel_bias, cache_k[layer], cache_v[layer], cache_kidx[layer],
                             q, qi, kw, kb, vb, n_seq, n_new)
    lru, buf_s, h_s = _rglru(u.reshape(n_seq, n_new, D_RNN), state_conv[layer], state_rnn[layer], *lru_w)
    y_sample = _merge_ffn(xs, attn, lru.reshape(n_seq * n_new, D_RNN), ga, gb, out_w,
                          g_ffn[layer], g_final).reshape(n_seq, n_new, D_MODEL)
    new_k_sample = k.reshape(1, n_seq, n_new, N_KV_HEADS, HEAD_DIM)
    new_v_sample = v.reshape(1, n_seq, n_new, N_KV_HEADS, HEAD_DIM)
    new_kidx_sample = kw[:, :IDX_DIM].reshape(1, n_seq, n_new, IDX_DIM)

    return (y_prompt, y_sample, new_k_prompt, new_v_prompt, new_kidx_prompt, buf_p[None], h_p[None],
            new_k_sample, new_v_sample, new_kidx_sample, buf_s[None], h_s[None])
```

```python
import functools
import math

import jax
import jax.numpy as jnp
from jax import lax
from jax.experimental import pallas as pl
from jax.experimental.pallas import tpu as pltpu

D_MODEL = 1024
N_HEADS = 8
N_KV_HEADS = 2
HEAD_DIM = 128
GROUP = N_HEADS // N_KV_HEADS
IDX_HEADS = 8
IDX_DIM = 64
TOPK_MAX = 256
D_RNN = D_MODEL
LRU_BLOCKS = 8
LRU_BLOCK_W = D_RNN // LRU_BLOCKS
CONV_W = 4
LRU_C = 8.0
N_BUCKETS = 32
MAX_EXACT = N_BUCKETS // 2
MAX_DISTANCE = 128
EPS = 1e-6
NEG_INF = -1e30
PAGE_SIZE = 128
KV_DIM = N_KV_HEADS * HEAD_DIM
LOG2E = math.log2(math.e)
Q_SCALE = HEAD_DIM ** -0.5 * LOG2E
WI_SCALE = IDX_HEADS ** -0.5 * IDX_DIM ** -0.5

LANES = 128
SUBLANES = 8
BF16_SUBLANES = 16
VMEM_LIMIT = 56 * 1024 * 1024

_BUCKET_STEPS = tuple(
    math.ceil(MAX_EXACT * (MAX_DISTANCE / MAX_EXACT) ** (k / (N_BUCKETS - MAX_EXACT)))
    for k in range(1, N_BUCKETS - MAX_EXACT))
M_INIT = -1e29
SCORE_BLOCKS_PER_TRIP = 4
ATTN_BLOCKS_PER_TRIP = 4
BOUND_SLACK = 1.0 + 2.0 ** -6
MIN_DENOMINATOR = 2.0 ** -60

_INT_MIN = -2 ** 31
_F32_MANTISSA_MASK = 0x007FFFFF
_INF_KEY = 0x7F800000 - _F32_MANTISSA_MASK
_BF16_MANTISSA_MASK = 0x7F
_INF_KEY16 = 0x7F80 - _BF16_MANTISSA_MASK
bf16 = jnp.bfloat16
f32 = jnp.float32
i32 = jnp.int32


def _dot(a, b):
    return jnp.dot(a, b, preferred_element_type=f32)


def _dot_nt(a, b):
    return lax.dot_general(a, b, (((1,), (1,)), ((), ())), preferred_element_type=f32)


def _const_spec(shape):
    zeros = (0,) * len(shape)
    return pl.BlockSpec(shape, lambda *_: zeros, pipeline_mode=pl.Buffered(1))


def _smem_spec():
    return pl.BlockSpec(memory_space=pltpu.SMEM)


def _key_to_float(key):
    mag = jnp.minimum(jnp.abs(jnp.maximum(key, -_INF_KEY)), _INF_KEY)
    bits = jnp.where(mag == 0, 0, mag + _F32_MANTISSA_MASK)
    return lax.bitcast_convert_type(jnp.where(key < 0, bits | _INT_MIN, bits), f32)


def _grid_to_float(key):
    mag = jnp.minimum(jnp.abs(jnp.maximum(key, -_INF_KEY16)), _INF_KEY16)
    bits = lax.shift_left(jnp.where(mag == 0, 0, mag + _BF16_MANTISSA_MASK), 16)
    return lax.bitcast_convert_type(jnp.where(key < 0, bits | _INT_MIN, bits), f32)


def _bucket(dist):
    d = jnp.maximum(dist, 0)
    large = jnp.full(d.shape, MAX_EXACT, i32)
    for step in _BUCKET_STEPS:
        large = large + (d >= step).astype(i32)
    return jnp.where(d < MAX_EXACT, d, large)


def _proj_kernel(x_ref, g_ref, wq_ref, wkv_ref, wqi_ref, wke_ref, wko_ref, wkw_ref, wu_ref,
                 wga_ref, wgb_ref,
                 q_ref, k_ref, v_ref, kb_ref, vb_ref, qi_ref, kie_ref, kio_ref, kw_ref,
                 u_ref, ga_ref, gb_ref):
    x = x_ref[...]
    ms = jnp.mean(x * x, axis=-1, keepdims=True)
    xn = ((x * lax.rsqrt(ms + EPS)) * g_ref[...]).astype(bf16)

    q_ref[...] = (_dot(xn, wq_ref[...]) * Q_SCALE).astype(bf16)
    kv = _dot(xn, wkv_ref[...])
    tm = x.shape[0]
    for n in range(N_KV_HEADS):
        rows = pl.ds(n, tm, stride=N_KV_HEADS)
        k_ref[rows, :] = kv[:, n * HEAD_DIM:(n + 1) * HEAD_DIM]
        v_ref[rows, :] = kv[:, KV_DIM + n * HEAD_DIM:KV_DIM + (n + 1) * HEAD_DIM]
    kb_ref[...] = kv[:, :KV_DIM].astype(bf16)
    vb_ref[...] = kv[:, KV_DIM:].astype(bf16)
    qi_ref[...] = _dot(xn, wqi_ref[...]).astype(bf16)
    kie_ref[...] = _dot(xn, wke_ref[...]).astype(bf16)
    kio_ref[...] = _dot(xn, wko_ref[...]).astype(bf16)
    kw = _dot(xn, wkw_ref[...])
    lane = lax.broadcasted_iota(i32, kw.shape, 1)
    kw_ref[...] = jnp.where(lane >= IDX_DIM, kw * WI_SCALE, kw)
    u_ref[...] = _dot(xn, wu_ref[...])
    ga_ref[...] = _dot(xn, wga_ref[...])
    gb_ref[...] = _dot(xn, wgb_ref[...])


def _split_w_in(w_in):
    sizes = (N_HEADS * HEAD_DIM, KV_DIM, KV_DIM, IDX_HEADS * IDX_DIM, IDX_DIM, IDX_HEADS,
             D_RNN, D_MODEL, D_MODEL)
    parts, acc = [], 0
    for s in sizes:
        parts.append(w_in[:, acc:acc + s])
        acc += s
    wq, wk, wv, wqi, wki, wwi, wu, wga, wgb = parts
    zk = jnp.zeros_like(wki)
    wkv = jnp.concatenate([wk, wv], axis=1)
    wke = jnp.concatenate([wki, zk], axis=1)
    wko = jnp.concatenate([zk, wki], axis=1)
    wkw = jnp.concatenate(
        [wki, wwi, jnp.zeros((w_in.shape[0], LANES - IDX_DIM - IDX_HEADS), w_in.dtype)], axis=1)
    return tuple(w.astype(bf16) for w in (wq, wkv, wqi, wke, wko, wkw, wu, wga, wgb))


def _proj(x2, g_mix, w_parts):
    n = x2.shape[0]
    tm = min(512, n)
    assert n % tm == 0

    def rows(width):
        return pl.BlockSpec((tm, width), lambda i: (i, 0))

    outs = ((1, N_HEADS * HEAD_DIM, bf16), (N_KV_HEADS, HEAD_DIM, f32), (N_KV_HEADS, HEAD_DIM, f32),
            (1, KV_DIM, bf16), (1, KV_DIM, bf16), (1, IDX_HEADS * IDX_DIM, bf16), (1, LANES, bf16),
            (1, LANES, bf16), (1, LANES, f32), (1, D_RNN, f32), (1, D_MODEL, f32), (1, D_MODEL, f32))
    return pl.pallas_call(
        _proj_kernel,
        grid=(n // tm,),
        in_specs=[rows(D_MODEL), _const_spec((1, D_MODEL))] + [_const_spec(w.shape) for w in w_parts],
        out_specs=[pl.BlockSpec((r * tm, w), lambda i: (i, 0)) for r, w, _ in outs],
        out_shape=[jax.ShapeDtypeStruct((r * n, w), d) for r, w, d in outs],
        compiler_params=pltpu.CompilerParams(
            dimension_semantics=("arbitrary",), vmem_limit_bytes=VMEM_LIMIT),
        name="proj",
    )(x2, g_mix.reshape(1, D_MODEL), *w_parts)


def _greedy_bits(count_at, n_bits, n_start, k):
    def body(it, carry):
        x, n_at = carry
        cand = x | lax.shift_left(jnp.int32(1), jnp.int32(n_bits - 1) - it)
        n = count_at(cand)
        ok = n >= k
        return jnp.where(ok, cand, x), jnp.where(ok, n, n_at)

    return lax.fori_loop(0, n_bits, body, (jnp.zeros(n_start.shape, i32), n_start))


def _kth_largest(count_ge, n_all, k):
    t_u, n_at = _greedy_bits(lambda u: count_ge(_key_to_float(u ^ _INT_MIN)), 32, n_all, k)
    return t_u ^ _INT_MIN, n_at


def _tie_cut(count_eq_below, shape, need, n_bits):
    def body(it, x):
        bit = lax.shift_left(jnp.int32(1), jnp.int32(n_bits - 1) - it)
        cand = x | bit
        cnt = count_eq_below(cand)
        return jnp.where(cnt < need, cand, x)

    return lax.fori_loop(0, n_bits, body, jnp.zeros(shape, i32))


def _prompt_attn_kernel(rb_ref, q_ref, qi_ref, kw_ref, kb_ref, vb_ref, kie_ref, kio_ref, o_ref,
                        score_ref, floor_ref, mask_ref, bias_ref, mx_ref, acc_ref, knorm_ref,
                        *, tq, topk, seq):
    b = pl.program_id(0)
    i = pl.program_id(1)
    tk = tq
    grid = jnp.bfloat16
    krow = lax.broadcasted_iota(i32, (tk, tq), 0)
    qcol = lax.broadcasted_iota(i32, (tk, tq), 1)

    @pl.when((b == 0) & (i == 0))
    def _():
        for t in range(2):
            bucket = _bucket(t * tq + krow - qcol)
            for h in range(N_HEADS):
                bias_ref[t * N_HEADS + h] = jnp.zeros((tq, tk), f32)

            def fill(bb, carry, t=t, bucket=bucket):
                hit = bucket == bb
                for h in range(N_HEADS):
                    bias_ref[t * N_HEADS + h] = jnp.where(
                        hit, rb_ref[bb, h] * LOG2E, bias_ref[t * N_HEADS + h])
                return carry

            lax.fori_loop(0, N_BUCKETS, fill, 0)

    w_t = kw_ref[...].T

    def score_block(j, causal):
        off = pl.multiple_of(j * tk, tk)
        ke = kie_ref[pl.ds(off, tk), :]
        ko = kio_ref[pl.ds(off, tk), :]
        s = jnp.zeros((tk, tq), f32)
        for p in range(IDX_HEADS // 2):
            rhs = qi_ref[:, p * LANES:(p + 1) * LANES]
            c0 = IDX_DIM + 2 * p
            s = s + jnp.maximum(_dot_nt(ke, rhs), 0.0) * w_t[c0:c0 + 1, :]
            s = s + jnp.maximum(_dot_nt(ko, rhs), 0.0) * w_t[c0 + 1:c0 + 2, :]
        if causal:
            s = jnp.where(krow <= qcol, s, NEG_INF)
        score_ref[pl.ds(off, tk), :] = s
        near = s.astype(grid)
        bits = pltpu.bitcast(near, jnp.int16)
        down = pltpu.bitcast(jnp.where(near > 0, bits - 1, bits + 1), grid)
        floor_ref[pl.ds(off, tk), :] = jnp.where(near.astype(f32) > s, down, near)

    def several_per_trip(n, fn, width=2):
        n = jnp.maximum(n, 0)
        done = jnp.int32(0)
        while width >= 1:
            trips = (n - done) // width

            def trip(p, carry, width=width, done=done):
                for w in range(width):
                    fn(done + width * p + w)
                return carry

            lax.fori_loop(0, trips, trip, 0)
            done = done + trips * width
            width //= 2

    several_per_trip(i, lambda j: score_block(j, False), width=SCORE_BLOCKS_PER_TRIP)
    score_block(i, True)

    def floor_count(t):
        t = t.astype(grid)

        def body(j, c):
            off = pl.multiple_of(j * tk, tk)
            hit = jnp.where(floor_ref[pl.ds(off, tk), :] >= t, jnp.ones((), jnp.int16), jnp.zeros((), jnp.int16))
            rows = [hit[g * BF16_SUBLANES:(g + 1) * BF16_SUBLANES] for g in range(tk // BF16_SUBLANES)]
            while len(rows) > 1:
                rows = [a + b for a, b in zip(rows[0::2], rows[1::2])]
            return c + rows[0]

        c = lax.fori_loop(0, i + 1, body, jnp.zeros((BF16_SUBLANES, tq), jnp.int16))
        return jnp.sum(c.astype(i32), axis=0, keepdims=True)


    def score_count(pred):
        def body(j, c):
            off = pl.multiple_of(j * tk, tk)
            hit = pred(score_ref[pl.ds(off, tk), :], off).astype(i32)
            rows = [hit[g * SUBLANES:(g + 1) * SUBLANES] for g in range(tk // SUBLANES)]
            while len(rows) > 1:
                rows = [a + b for a, b in zip(rows[0::2], rows[1::2])]
            return c + rows[0]

        c = lax.fori_loop(0, i + 1, body, jnp.zeros((SUBLANES, tq), i32))
        return jnp.sum(c, axis=0, keepdims=True)

    def count_ge(t):
        return score_count(lambda sc, off: sc >= t)

    n_all = jnp.full((1, tq), (i + 1) * tk, i32)
    lo_key, n_lo = _greedy_bits(lambda key: floor_count(_grid_to_float(key - 2 ** 15)), 16, n_all, topk)
    lo = _grid_to_float(lo_key - 2 ** 15)
    hi = _grid_to_float(lo_key - 2 ** 15 + 1)
    width = hi - lo
    step = jnp.where(width < jnp.inf, width, 0.0) * 2.0 ** -16
    pos, n_ge = _greedy_bits(lambda j: count_ge(lo + j.astype(f32) * step), 16, n_lo, topk)
    thr = lo + pos.astype(f32) * step

    @pl.when(jnp.max(n_ge) > topk)
    def _():
        need = topk - score_count(lambda sc, off: sc > thr)
        cut = _tie_cut(lambda x: score_count(lambda sc, off: (sc == thr) & (krow + off < x)),
                       (1, tq), need, (seq - 1).bit_length())

        def demote(j, carry):
            off = pl.multiple_of(j * tk, tk)
            sc = score_ref[pl.ds(off, tk), :]
            score_ref[pl.ds(off, tk), :] = jnp.where((sc == thr) & (krow + off > cut), -jnp.inf, sc)
            return carry

        lax.fori_loop(0, i + 1, demote, 0)

    far_bias = [rb_ref[N_BUCKETS - 1, h] * LOG2E for h in range(N_HEADS)]

    def to_mask(j, causal):
        off = pl.multiple_of(j * tk, tk)
        sel = score_ref[pl.ds(off, tk), :] >= thr
        if causal:
            sel = sel & (krow <= qcol)
        mask_ref[:, pl.ds(off, tk)] = jnp.where(sel, 0.0, NEG_INF).T

    several_per_trip(i, lambda j: to_mask(j, False))
    to_mask(i, True)

    def group_logits(j, mode):
        off = pl.multiple_of(j * tk, tk)
        mask = mask_ref[:, pl.ds(off, tk)]
        for n in range(N_KV_HEADS):
            ns = slice(n * HEAD_DIM, (n + 1) * HEAD_DIM)
            heads = range(n * GROUP, (n + 1) * GROUP)
            qs = jnp.concatenate([q_ref[:, h * HEAD_DIM:(h + 1) * HEAD_DIM] for h in heads], axis=0)
            s = _dot_nt(qs, kb_ref[pl.ds(off, tk), ns])
            per_head = []
            for g, h in enumerate(heads):
                sg = s[g * tq:(g + 1) * tq] + mask
                if mode != 0:
                    sg = sg + bias_ref[(2 - mode) * N_HEADS + h]
                per_head.append((h, sg))
            yield n, off, ns, per_head

    def over_blocks(block_fn):
        several_per_trip(i - 1, lambda j: block_fn(j, 0), width=ATTN_BLOCKS_PER_TRIP)

        @pl.when(i >= 1)
        def _():
            block_fn(i - 1, 1)

        block_fn(i, 2)

    def exact_max():
        mx_ref[...] = jnp.full(mx_ref.shape, NEG_INF, f32)

        def max_block(j, mode):
            for _, _, _, per_head in group_logits(j, mode):
                for h, sg in per_head:
                    part = sg[:, 0:LANES]
                    for c in range(1, tk // LANES):
                        part = jnp.maximum(part, sg[:, c * LANES:(c + 1) * LANES])
                    if mode == 0:
                        part = part + far_bias[h]
                    mx_ref[h] = jnp.maximum(mx_ref[h], part)

        over_blocks(max_block)
        for h in range(N_HEADS):
            mx_ref[h] = jnp.broadcast_to(jnp.max(mx_ref[h], axis=-1, keepdims=True), (tq, LANES))

    ones_cols = jnp.ones((tk, HEAD_DIM), bf16)

    def value_sweep():
        acc_ref[...] = jnp.zeros(acc_ref.shape, f32)

        def pv_block(j, mode):
            for n, off, ns, per_head in group_logits(j, mode):
                ps = []
                for h, sg in per_head:
                    shift = mx_ref[h] - far_bias[h] if mode == 0 else mx_ref[h]
                    shift = jnp.concatenate([shift] * (tk // LANES), axis=1)
                    ps.append(jnp.exp2(sg - shift).astype(bf16))
                values = jnp.concatenate([vb_ref[pl.ds(off, tk), ns], ones_cols], axis=1)
                acc_ref[n] = acc_ref[n] + _dot(jnp.concatenate(ps, axis=0), values)

        over_blocks(pv_block)

    @pl.when(i == 0)
    def _():
        knorm_ref[...] = jnp.zeros(knorm_ref.shape, f32)

    diag = pl.multiple_of(i * tk, tk)
    for n in range(N_KV_HEADS):
        kf = kb_ref[pl.ds(diag, tk), n * HEAD_DIM:(n + 1) * HEAD_DIM].astype(f32)
        ksq = jnp.max(jnp.sum(kf * kf, axis=-1, keepdims=True), axis=0, keepdims=True)
        knorm_ref[n] = jnp.maximum(knorm_ref[n], jnp.broadcast_to(ksq, knorm_ref.shape[1:]))
    for h in range(N_HEADS):
        qf = q_ref[:, h * HEAD_DIM:(h + 1) * HEAD_DIM].astype(f32)
        qsq = jnp.sum(qf * qf, axis=-1, keepdims=True)
        bias_max = lax.fori_loop(1, N_BUCKETS, lambda bb, m, h=h: jnp.maximum(m, rb_ref[bb, h]),
                                 rb_ref[0, h]) * LOG2E
        bound = jnp.sqrt(qsq * knorm_ref[h // GROUP][0:1, 0:1]) * BOUND_SLACK + bias_max
        mx_ref[h] = jnp.broadcast_to(bound, (tq, LANES))
    value_sweep()

    @pl.when(jnp.min(acc_ref[:, :, HEAD_DIM:]) < MIN_DENOMINATOR)
    def _():
        exact_max()
        value_sweep()

    for h in range(N_HEADS):
        acc = acc_ref[h // GROUP, (h % GROUP) * tq:(h % GROUP + 1) * tq, :]
        out = acc[:, :HEAD_DIM] / acc[:, HEAD_DIM:]
        o_ref[:, h * HEAD_DIM:(h + 1) * HEAD_DIM] = out.astype(o_ref.dtype)


def _prompt_attn(rel_bias, q, qi, kw, kb, vb, kie, kio, batch, seq):
    tq = min(256, seq)
    assert seq % tq == 0 and tq % LANES == 0 and tq >= MAX_DISTANCE
    nq = seq // tq
    topk = min(TOPK_MAX, seq // 4)

    def qrows(width):
        return pl.BlockSpec((tq, width), lambda b, i: (b * nq + i, 0))

    def krows(width):
        return pl.BlockSpec((seq, width), lambda b, i: (b, 0))

    kern = functools.partial(_prompt_attn_kernel, tq=tq, topk=topk, seq=seq)
    return pl.pallas_call(
        kern,
        grid=(batch, nq),
        in_specs=[_smem_spec(), qrows(N_HEADS * HEAD_DIM), qrows(IDX_HEADS * IDX_DIM), qrows(LANES),
                  krows(KV_DIM), krows(KV_DIM), krows(LANES), krows(LANES)],
        out_specs=qrows(N_HEADS * HEAD_DIM),
        out_shape=jax.ShapeDtypeStruct((batch * seq, N_HEADS * HEAD_DIM), bf16),
        scratch_shapes=[
            pltpu.VMEM((seq, tq), f32),
            pltpu.VMEM((seq, tq), jnp.bfloat16),
            pltpu.VMEM((tq, seq), f32),
            pltpu.VMEM((2 * N_HEADS, tq, tq), f32),
            pltpu.VMEM((N_HEADS, tq, LANES), f32),
            pltpu.VMEM((N_KV_HEADS, GROUP * tq, 2 * HEAD_DIM), f32),
            pltpu.VMEM((N_KV_HEADS, SUBLANES, LANES), f32),
        ],
        compiler_params=pltpu.CompilerParams(
            dimension_semantics=("arbitrary", "arbitrary"), vmem_limit_bytes=VMEM_LIMIT),
        name="prompt_attn",
    )(rel_bias, q, qi, kw, kb, vb, kie, kio)


def _fetch_pages(pt_ref, pairs, sem_ref, pages_per_step):
    n_steps = pl.num_programs(1)
    t = pl.program_id(0) * n_steps + pl.program_id(1)
    slot = t % 2

    def copies(step, slot_):
        seq, chunk = step // n_steps, step % n_steps
        for pp in range(pages_per_step):
            page = pt_ref[seq, chunk * pages_per_step + pp]
            for a, (hbm_ref, buf_ref) in enumerate(pairs):
                yield pltpu.make_async_copy(hbm_ref.at[page], buf_ref.at[slot_, pp], sem_ref.at[a, slot_, pp])

    @pl.when(t == 0)
    def _():
        for cp in copies(t, slot):
            cp.start()

    @pl.when(t + 1 < pl.num_programs(0) * n_steps)
    def _():
        for cp in copies(t + 1, 1 - slot):
            cp.start()

    for cp in copies(t, slot):
        cp.wait()
    return slot


def _sample_index_kernel(pt_ref, qh_ref, wc_ref, kin_ref, cki_hbm, past_ref, new_ref, buf_ref, sem_ref,
                         *, pages_per_step, n_new):
    slot = _fetch_pages(pt_ref, [(cki_hbm, buf_ref)], sem_ref, pages_per_step)
    c = pl.program_id(1)
    qh = qh_ref[...]
    wc = wc_ref[...]

    def scores(keys_t):
        s = jnp.maximum(_dot(qh, keys_t), 0.0) * wc
        out = s[0:n_new]
        for h in range(1, IDX_HEADS):
            out = out + s[h * n_new:(h + 1) * n_new]
        return out

    pages = jnp.concatenate([buf_ref[slot, pp].astype(bf16) for pp in range(pages_per_step)], axis=1)
    past_ref[...] = scores(pages)

    @pl.when(c == 0)
    def _():
        sc = scores(kin_ref[...])
        tok = lax.broadcasted_iota(i32, sc.shape, 0)
        key = lax.broadcasted_iota(i32, sc.shape, 1)
        new_ref[...] = jnp.where(key <= tok, sc, NEG_INF)


def _sample_attn_kernel(pt_ref, rb_ref, past_ref, new_ref, q_ref, kn_ref, vn_ref, ck_hbm, cv_hbm, o_ref,
                        score_ref, kcat_ref, vcat_ref, thr_ref, m_ref, l_ref, acc_ref,
                        kbuf_ref, vbuf_ref, sem_ref, *, pages_per_step, n_new, n_past, topk):
    P = pages_per_step
    slot = _fetch_pages(pt_ref, [(ck_hbm, kbuf_ref), (cv_hbm, vbuf_ref)], sem_ref, P)
    c = pl.program_id(1)
    n_steps = pl.num_programs(1)
    n_rows = N_HEADS * n_new
    half = GROUP * n_new
    step_keys = P * PAGE_SIZE
    n_keys = n_past + LANES

    def head_column(bucket_row):
        r = lax.broadcasted_iota(i32, (n_rows, 1), 0) // n_new
        out = jnp.zeros((n_rows, 1), f32)
        for h in range(N_HEADS):
            out = jnp.where(r == h, rb_ref[bucket_row, h] * LOG2E, out)
        return out

    def bias_tile(dist):
        bucket = _bucket(dist)

        def fill(bb, acc):
            return jnp.where(bucket == bb, head_column(bb), acc)

        return lax.fori_loop(0, N_BUCKETS, fill, jnp.zeros(dist.shape, f32))

    @pl.when(c == 0)
    def _():
        score_ref[:, :n_past] = past_ref[...]
        score_ref[:, n_past:] = new_ref[...]

        def count(pred):
            return jnp.sum(pred(score_ref[...]).astype(i32), axis=-1, keepdims=True)

        def count_ge(t):
            return count(lambda sc: sc >= t)

        thr_key, n_ge = _kth_largest(count_ge, jnp.full((n_new, 1), n_keys, i32), topk)
        thr = _key_to_float(thr_key)
        thr_ref[...] = thr

        @pl.when(jnp.max(n_ge) > topk)
        def _():
            need = topk - count_ge(_key_to_float(thr_key + 1))
            colk = lax.broadcasted_iota(i32, (n_new, n_keys), 1)
            cut = _tie_cut(lambda x: count(lambda sc: (sc == thr) & (colk < x)),
                           (n_new, 1), need, (n_keys - 1).bit_length())
            sc = score_ref[...]
            score_ref[...] = jnp.where((sc == thr) & (colk > cut), _key_to_float(thr_key - 1), sc)

        m_ref[...] = jnp.full(m_ref.shape, M_INIT, f32)
        l_ref[...] = jnp.zeros(l_ref.shape, f32)
        acc_ref[...] = jnp.zeros(acc_ref.shape, f32)

    thr = thr_ref[...]
    far_bias = head_column(N_BUCKETS - 1)
    tok = lax.broadcasted_iota(i32, (n_rows, LANES), 0) % n_new
    kpos = lax.broadcasted_iota(i32, (n_rows, LANES), 1)

    def online_update(n, s, values):
        rows = slice(n * half, (n + 1) * half)
        m_old = m_ref[rows]
        m_new = jnp.maximum(m_old, jnp.max(s, axis=-1, keepdims=True))
        alpha = jnp.exp2(m_old - m_new)
        p = jnp.exp2(s - m_new)
        l_ref[rows] = alpha * l_ref[rows] + jnp.sum(p, axis=-1, keepdims=True)
        acc_ref[rows] = alpha * acc_ref[rows] + _dot(p.astype(bf16), values)
        m_ref[rows] = m_new

    def tiled_mask(scores):
        mask = jnp.where(scores >= thr, 0.0, NEG_INF)
        return jnp.concatenate([mask] * GROUP, axis=0)

    for pp in range(P):
        for n in range(N_KV_HEADS):
            rows = pl.ds(n, PAGE_SIZE, stride=N_KV_HEADS)
            kcat_ref[n, pp * PAGE_SIZE:(pp + 1) * PAGE_SIZE, :] = kbuf_ref[slot, pp, rows, :].astype(bf16)
            vcat_ref[n, pp * PAGE_SIZE:(pp + 1) * PAGE_SIZE, :] = vbuf_ref[slot, pp, rows, :].astype(bf16)

    base = pl.multiple_of(c * step_keys, step_keys)
    mask = tiled_mask(score_ref[:, pl.ds(base, step_keys)])
    last_bias = lax.cond(c == n_steps - 1,
                         lambda: bias_tile(PAGE_SIZE + tok - kpos),
                         lambda: jnp.broadcast_to(far_bias, (n_rows, LANES)))
    for n in range(N_KV_HEADS):
        rows = slice(n * half, (n + 1) * half)
        s = _dot_nt(q_ref[rows, :], kcat_ref[n])
        s = jnp.concatenate([s[:, :step_keys - PAGE_SIZE] + far_bias[rows],
                             s[:, step_keys - PAGE_SIZE:] + last_bias[rows]], axis=1)
        online_update(n, s + mask, vcat_ref[n])

    @pl.when(c == n_steps - 1)
    def _():
        causal = (lax.broadcasted_iota(i32, (half, LANES), 1)
                  <= lax.broadcasted_iota(i32, (half, LANES), 0) % n_new)
        mask = jnp.where(causal, tiled_mask(score_ref[:, n_past:]), NEG_INF)
        bias = bias_tile(tok - kpos)
        for n in range(N_KV_HEADS):
            rows = slice(n * half, (n + 1) * half)
            ns = slice(n * HEAD_DIM, (n + 1) * HEAD_DIM)
            s = _dot_nt(q_ref[rows, :], kn_ref[:, ns]) + bias[rows]
            online_update(n, s + mask, vn_ref[:, ns])
        o_ref[...] = acc_ref[...] / l_ref[...]


def _sample_attention(page_table, rel_bias, cache_k, cache_v, cache_kidx, q, qi, kw, kb, vb, n_seq, n_new):
    n_pages = page_table.shape[1]
    n_pool = cache_k.shape[0]
    n_past = n_pages * PAGE_SIZE
    topk = min(TOPK_MAX, (n_past + n_new) // 4)
    P = min(32, n_pages)
    assert n_pages % P == 0 and n_new == SUBLANES
    n_steps = n_pages // P
    n_rows = N_HEADS * n_new
    page_cols = N_KV_HEADS * PAGE_SIZE

    qh = qi.reshape(n_seq, n_new, IDX_HEADS, IDX_DIM).transpose(0, 2, 1, 3).reshape(n_seq, n_rows, IDX_DIM)
    wc = kw[:, IDX_DIM:IDX_DIM + IDX_HEADS].reshape(n_seq, n_new, IDX_HEADS).transpose(0, 2, 1)
    wc = wc.reshape(n_seq, n_rows, 1)
    kin = kw[:, :IDX_DIM].astype(bf16).reshape(n_seq, n_new, IDX_DIM).transpose(0, 2, 1)
    kin = jnp.pad(kin, ((0, 0), (0, 0), (0, LANES - n_new)))
    qs = q.reshape(n_seq, n_new, N_HEADS, HEAD_DIM).transpose(0, 2, 1, 3).reshape(n_seq, n_rows, HEAD_DIM)
    pad = ((0, 0), (0, LANES - n_new), (0, 0))
    kn = jnp.pad(kb.reshape(n_seq, n_new, KV_DIM), pad)
    vn = jnp.pad(vb.reshape(n_seq, n_new, KV_DIM), pad)
    ck = cache_k.reshape(n_pool, page_cols, HEAD_DIM)
    cv = cache_v.reshape(n_pool, page_cols, HEAD_DIM)
    cki = jnp.swapaxes(cache_kidx, -1, -2)

    def seq_block(shape):
        nd = len(shape)
        return pl.BlockSpec((None,) + shape, lambda b, c, pt: (b,) + (0,) * nd)

    paged = pl.BlockSpec(memory_space=pl.ANY)

    past_keys, new_keys = pl.pallas_call(
        functools.partial(_sample_index_kernel, pages_per_step=P, n_new=n_new),
        grid_spec=pltpu.PrefetchScalarGridSpec(
            num_scalar_prefetch=1,
            grid=(n_seq, n_steps),
            in_specs=[seq_block((n_rows, IDX_DIM)), seq_block((n_rows, 1)), seq_block((IDX_DIM, LANES)),
                      paged],
            out_specs=[pl.BlockSpec((None, n_new, P * PAGE_SIZE), lambda b, c, pt: (b, 0, c)),
                       seq_block((n_new, LANES))],
            scratch_shapes=[pltpu.VMEM((2, P, IDX_DIM, PAGE_SIZE), f32),
                            pltpu.SemaphoreType.DMA((1, 2, P))],
        ),
        out_shape=[jax.ShapeDtypeStruct((n_seq, n_new, n_past), f32),
                   jax.ShapeDtypeStruct((n_seq, n_new, LANES), f32)],
        compiler_params=pltpu.CompilerParams(
            dimension_semantics=("arbitrary", "arbitrary"), vmem_limit_bytes=VMEM_LIMIT),
        name="sample_index",
    )(page_table, qh, wc, kin, cki)

    out = pl.pallas_call(
        functools.partial(_sample_attn_kernel, pages_per_step=P, n_new=n_new, n_past=n_past, topk=topk),
        grid_spec=pltpu.PrefetchScalarGridSpec(
            num_scalar_prefetch=1,
            grid=(n_seq, n_steps),
            in_specs=[_smem_spec(), seq_block((n_new, n_past)), seq_block((n_new, LANES)),
                      seq_block((n_rows, HEAD_DIM)), seq_block((LANES, KV_DIM)), seq_block((LANES, KV_DIM)),
                      paged, paged],
            out_specs=seq_block((n_rows, HEAD_DIM)),
            scratch_shapes=[
                pltpu.VMEM((n_new, n_past + LANES), f32),
                pltpu.VMEM((N_KV_HEADS, P * PAGE_SIZE, HEAD_DIM), bf16),
                pltpu.VMEM((N_KV_HEADS, P * PAGE_SIZE, HEAD_DIM), bf16),
                pltpu.VMEM((n_new, 1), f32),
                pltpu.VMEM((n_rows, 1), f32),
                pltpu.VMEM((n_rows, 1), f32),
                pltpu.VMEM((n_rows, HEAD_DIM), f32),
                pltpu.VMEM((2, P, page_cols, HEAD_DIM), f32),
                pltpu.VMEM((2, P, page_cols, HEAD_DIM), f32),
                pltpu.SemaphoreType.DMA((2, 2, P)),
            ],
        ),
        out_shape=jax.ShapeDtypeStruct((n_seq, n_rows, HEAD_DIM), f32),
        compiler_params=pltpu.CompilerParams(
            dimension_semantics=("arbitrary", "arbitrary"), vmem_limit_bytes=VMEM_LIMIT),
        name="sample_attn",
    )(page_table, rel_bias, past_keys, new_keys, qs, kn, vn, ck, cv)

    out = out.reshape(n_seq, N_HEADS, n_new, HEAD_DIM).transpose(0, 2, 1, 3)
    return out.reshape(n_seq * n_new, N_HEADS * HEAD_DIM).astype(bf16)


def _rglru_kernel(u_ref, buf_ref, h0_ref, cw_ref, cb_ref, wg_ref, bg_ref, lam_ref,
                  y_ref, nbuf_ref, hT_ref,
                  ext_ref, a_ref, b_ref, hs_ref, h_ref, *, tt):
    t = pl.program_id(1)
    head = SUBLANES

    @pl.when(t == 0)
    def _():
        ext_ref[head - (CONV_W - 1):head, :] = buf_ref[...]
        h_ref[...] = h0_ref[...]

    ext_ref[head:head + tt, :] = u_ref[...]
    xc = cb_ref[...] + ext_ref[head - 3:head - 3 + tt, :] * cw_ref[0:1, :]
    for j in range(1, CONV_W):
        xc = xc + ext_ref[head - 3 + j:head - 3 + j + tt, :] * cw_ref[j:j + 1, :]
    tail = ext_ref[head + tt - (CONV_W - 1):head + tt, :]
    nbuf_ref[...] = tail
    ext_ref[head - (CONV_W - 1):head, :] = tail

    lam = lam_ref[...]
    neg = -lam
    softplus = jnp.maximum(neg, 0.0) + jnp.log1p(jnp.exp(-jnp.abs(neg)))
    for n in range(LRU_BLOCKS):
        ns = slice(n * LRU_BLOCK_W, (n + 1) * LRU_BLOCK_W)
        xn = xc[:, ns]
        gates = _dot(xn.astype(bf16), wg_ref[n]) + bg_ref[n]
        r = jax.nn.sigmoid(gates[:, :LRU_BLOCK_W])
        ig = jax.nn.sigmoid(gates[:, LRU_BLOCK_W:])
        log_a = (-LRU_C) * r * softplus[:, ns]
        a = jnp.exp(log_a)
        a_ref[:, ns] = a
        b_ref[:, ns] = jnp.sqrt(-jnp.tanh(log_a) * (1.0 + a * a)) * (ig * xn)

    def step(r_, h):
        h = a_ref[pl.ds(r_, 1), :] * h + b_ref[pl.ds(r_, 1), :]
        hs_ref[pl.ds(r_, 1), :] = h
        return h

    h = lax.fori_loop(0, tt, step, h_ref[...], unroll=8)
    h_ref[...] = h
    hT_ref[...] = h
    y_ref[...] = hs_ref[...].astype(y_ref.dtype)


def _rglru(u, conv_buf, h0, conv_w, conv_b, w_rg, b_rg, w_ig, b_ig, lam):
    nb, seq, _ = u.shape
    tt = min(256, seq)
    assert seq % tt == 0 and tt % SUBLANES == 0 and tt >= CONV_W - 1
    wg = jnp.concatenate([w_rg, w_ig], axis=-1).astype(bf16)
    bg = jnp.concatenate([b_rg, b_ig], axis=-1).reshape(LRU_BLOCKS, 1, 2 * LRU_BLOCK_W)

    def per_seq(rows):
        return pl.BlockSpec((None, rows, D_RNN), lambda b, t: (b, 0, 0))

    y, nbuf, hT = pl.pallas_call(
        functools.partial(_rglru_kernel, tt=tt),
        grid=(nb, seq // tt),
        in_specs=[pl.BlockSpec((None, tt, D_RNN), lambda b, t: (b, t, 0)),
                  per_seq(CONV_W - 1), per_seq(1),
                  _const_spec((CONV_W, D_RNN)), _const_spec((1, D_RNN)),
                  _const_spec(wg.shape), _const_spec(bg.shape), _const_spec((1, D_RNN))],
        out_specs=[pl.BlockSpec((None, tt, D_RNN), lambda b, t: (b, t, 0)),
                   per_seq(CONV_W - 1), per_seq(1)],
        out_shape=[jax.ShapeDtypeStruct((nb, seq, D_RNN), bf16),
                   jax.ShapeDtypeStruct((nb, CONV_W - 1, D_RNN), f32),
                   jax.ShapeDtypeStruct((nb, 1, D_RNN), f32)],
        scratch_shapes=[
            pltpu.VMEM((SUBLANES + tt, D_RNN), f32),
            pltpu.VMEM((tt, D_RNN), f32),
            pltpu.VMEM((tt, D_RNN), f32),
            pltpu.VMEM((tt, D_RNN), f32),
            pltpu.VMEM((1, D_RNN), f32),
        ],
        compiler_params=pltpu.CompilerParams(
            dimension_semantics=("arbitrary", "arbitrary"), vmem_limit_bytes=VMEM_LIMIT),
        name="rglru",
    )(u, conv_buf, h0.reshape(nb, 1, D_RNN), conv_w, conv_b.reshape(1, D_RNN), wg, bg,
      lam.reshape(1, D_RNN))
    return y, nbuf, hT.reshape(nb, D_RNN)


def _merge_ffn_kernel(x_ref, attn_ref, lru_ref, ga_ref, gb_ref, woa_ref, wol_ref, wout_ref,
                      gf_ref, wfg_ref, wfu_ref, wfd_ref, gfin_ref, y_ref):
    merged = (jax.nn.sigmoid(ga_ref[...]) * _dot(attn_ref[...], woa_ref[...])
              + jax.nn.sigmoid(gb_ref[...]) * _dot(lru_ref[...], wol_ref[...]))
    h = x_ref[...] + _dot(merged.astype(bf16), wout_ref[...])
    hn = (h * lax.rsqrt(jnp.mean(h * h, axis=-1, keepdims=True) + EPS)) * gf_ref[...]
    hn = hn.astype(bf16)
    act = jax.nn.silu(_dot(hn, wfg_ref[...])) * _dot(hn, wfu_ref[...])
    y = h + _dot(act.astype(bf16), wfd_ref[...])
    y_ref[...] = (y * lax.rsqrt(jnp.mean(y * y, axis=-1, keepdims=True) + EPS)) * gfin_ref[...]


def _merge_ffn(x2, attn, lru, ga, gb, weights, g_ffn, g_final):
    n = x2.shape[0]
    tm = min(256, n)
    assert n % tm == 0
    woa, wol, wout, wfg, wfu, wfd = weights

    def rows(width):
        return pl.BlockSpec((tm, width), lambda i: (i, 0))

    return pl.pallas_call(
        _merge_ffn_kernel,
        grid=(n // tm,),
        in_specs=[rows(D_MODEL), rows(N_HEADS * HEAD_DIM), rows(D_RNN), rows(D_MODEL), rows(D_MODEL),
                  _const_spec(woa.shape), _const_spec(wol.shape), _const_spec(wout.shape),
                  _const_spec((1, D_MODEL)), _const_spec(wfg.shape), _const_spec(wfu.shape),
                  _const_spec(wfd.shape), _const_spec((1, D_MODEL))],
        out_specs=rows(D_MODEL),
        out_shape=jax.ShapeDtypeStruct((n, D_MODEL), f32),
        compiler_params=pltpu.CompilerParams(
            dimension_semantics=("arbitrary",), vmem_limit_bytes=VMEM_LIMIT),
        name="merge_ffn",
    )(x2, attn, lru, ga, gb, woa, wol, wout, g_ffn.reshape(1, D_MODEL), wfg, wfu, wfd,
      g_final.reshape(1, D_MODEL))


def kernel(x_prompt, x_sample, cache_k, cache_v, cache_kidx, state_conv, state_rnn, page_table,
           rel_bias, g_mix, w_in, conv_w, conv_b, w_rgate, b_rgate, w_igate, b_igate, lru_lambda,
           w_o_attn, w_o_lru, w_out, g_ffn, w_ffn_gate, w_ffn_up, w_ffn_down, g_final):
    assert w_in.shape[0] == 1, "one trunk layer"
    batch, seq, _ = x_prompt.shape
    n_seq, n_new, _ = x_sample.shape
    layer = 0

    w_parts = _split_w_in(w_in[layer])
    lru_w = (conv_w[layer], conv_b[layer], w_rgate[layer], b_rgate[layer], w_igate[layer],
             b_igate[layer], lru_lambda[layer])
    out_w = tuple(w[layer].astype(bf16)
                  for w in (w_o_attn, w_o_lru, w_out, w_ffn_gate, w_ffn_up, w_ffn_down))

    xp = x_prompt.reshape(batch * seq, D_MODEL)
    q, k, v, kb, vb, qi, kie, kio, kw, u, ga, gb = _proj(xp, g_mix[layer], w_parts)
    attn = _prompt_attn(rel_bias, q, qi, kw, kb, vb, kie, kio, batch, seq)
    lru, buf_p, h_p = _rglru(u.reshape(batch, seq, D_RNN),
                             jnp.zeros((batch, CONV_W - 1, D_RNN), f32),
                             jnp.zeros((batch, D_RNN), f32), *lru_w)
    y_prompt = _merge_ffn(xp, attn, lru.reshape(batch * seq, D_RNN), ga, gb, out_w,
                          g_ffn[layer], g_final).reshape(batch, seq, D_MODEL)
    new_k_prompt = k.reshape(1, batch, seq, N_KV_HEADS, HEAD_DIM)
    new_v_prompt = v.reshape(1, batch, seq, N_KV_HEADS, HEAD_DIM)
    new_kidx_prompt = kw[:, :IDX_DIM].reshape(1, batch, seq, IDX_DIM)

    xs = x_sample.reshape(n_seq * n_new, D_MODEL)
    q, k, v, kb, vb, qi, kie, kio, kw, u, ga, gb = _proj(xs, g_mix[layer], w_parts)
    attn = _sample_attention(page_table, rel_bias, cache_k[layer], cache_v[layer], cache_kidx[layer],
                             q, qi, kw, kb, vb, n_seq, n_new)
    lru, buf_s, h_s = _rglru(u.reshape(n_seq, n_new, D_RNN), state_conv[layer], state_rnn[layer], *lru_w)
    y_sample = _merge_ffn(xs, attn, lru.reshape(n_seq * n_new, D_RNN), ga, gb, out_w,
                          g_ffn[layer], g_final).reshape(n_seq, n_new, D_MODEL)
    new_k_sample = k.reshape(1, n_seq, n_new, N_KV_HEADS, HEAD_DIM)
    new_v_sample = v.reshape(1, n_seq, n_new, N_KV_HEADS, HEAD_DIM)
    new_kidx_sample = kw[:, :IDX_DIM].reshape(1, n_seq, n_new, IDX_DIM)

    return (y_prompt, y_sample, new_k_prompt, new_v_prompt, new_kidx_prompt, buf_p[None], h_p[None],
            new_k_sample, new_v_sample, new_kidx_sample, buf_s[None], h_s[None])
```

```python
import functools
import math

import jax
import jax.numpy as jnp
from jax import lax
from jax.experimental import pallas as pl
from jax.experimental.pallas import tpu as pltpu

D_MODEL = 1024
N_HEADS = 8
N_KV_HEADS = 2
HEAD_DIM = 128
GROUP = N_HEADS // N_KV_HEADS
IDX_HEADS = 8
IDX_DIM = 64
TOPK_MAX = 256
D_RNN = D_MODEL
LRU_BLOCKS = 8
LRU_BLOCK_W = D_RNN // LRU_BLOCKS
CONV_W = 4
LRU_C = 8.0
N_BUCKETS = 32
MAX_EXACT = N_BUCKETS // 2
MAX_DISTANCE = 128
EPS = 1e-6
NEG_INF = -1e30
PAGE_SIZE = 128
KV_DIM = N_KV_HEADS * HEAD_DIM
LOG2E = math.log2(math.e)
Q_SCALE = HEAD_DIM ** -0.5 * LOG2E
WI_SCALE = IDX_HEADS ** -0.5 * IDX_DIM ** -0.5

LANES = 128
SUBLANES = 8
BF16_SUBLANES = 16
VMEM_LIMIT = 56 * 1024 * 1024

_BUCKET_STEPS = tuple(
    math.ceil(MAX_EXACT * (MAX_DISTANCE / MAX_EXACT) ** (k / (N_BUCKETS - MAX_EXACT)))
    for k in range(1, N_BUCKETS - MAX_EXACT))
M_INIT = -1e29
SCORE_BLOCKS_PER_TRIP = 4
ATTN_BLOCKS_PER_TRIP = 4
BOUND_SLACK = 1.0 + 2.0 ** -6
MIN_DENOMINATOR = 2.0 ** -60

_INT_MIN = -2 ** 31
_F32_MANTISSA_MASK = 0x007FFFFF
_INF_KEY = 0x7F800000 - _F32_MANTISSA_MASK
_BF16_MANTISSA_MASK = 0x7F
_INF_KEY16 = 0x7F80 - _BF16_MANTISSA_MASK
bf16 = jnp.bfloat16
f32 = jnp.float32
i32 = jnp.int32


def _dot(a, b):
    return jnp.dot(a, b, preferred_element_type=f32)


def _dot_nt(a, b):
    return lax.dot_general(a, b, (((1,), (1,)), ((), ())), preferred_element_type=f32)


def _const_spec(shape):
    zeros = (0,) * len(shape)
    return pl.BlockSpec(shape, lambda *_: zeros, pipeline_mode=pl.Buffered(1))


def _smem_spec():
    return pl.BlockSpec(memory_space=pltpu.SMEM)


def _key_to_float(key):
    mag = jnp.minimum(jnp.abs(jnp.maximum(key, -_INF_KEY)), _INF_KEY)
    bits = jnp.where(mag == 0, 0, mag + _F32_MANTISSA_MASK)
    return lax.bitcast_convert_type(jnp.where(key < 0, bits | _INT_MIN, bits), f32)


def _grid_to_float(key):
    mag = jnp.minimum(jnp.abs(jnp.maximum(key, -_INF_KEY16)), _INF_KEY16)
    bits = lax.shift_left(jnp.where(mag == 0, 0, mag + _BF16_MANTISSA_MASK), 16)
    return lax.bitcast_convert_type(jnp.where(key < 0, bits | _INT_MIN, bits), f32)


def _bucket(dist):
    d = jnp.maximum(dist, 0)
    large = jnp.full(d.shape, MAX_EXACT, i32)
    for step in _BUCKET_STEPS:
        large = large + (d >= step).astype(i32)
    return jnp.where(d < MAX_EXACT, d, large)


def _proj_kernel(x_ref, g_ref, wq_ref, wkv_ref, wqi_ref, wke_ref, wko_ref, wkw_ref, wu_ref,
                 wga_ref, wgb_ref,
                 q_ref, k_ref, v_ref, kb_ref, vb_ref, qi_ref, kie_ref, kio_ref, kw_ref,
                 u_ref, ga_ref, gb_ref):
    x = x_ref[...]
    ms = jnp.mean(x * x, axis=-1, keepdims=True)
    xn = ((x * lax.rsqrt(ms + EPS)) * g_ref[...]).astype(bf16)

    q_ref[...] = (_dot(xn, wq_ref[...]) * Q_SCALE).astype(bf16)
    kv = _dot(xn, wkv_ref[...])
    tm = x.shape[0]
    for n in range(N_KV_HEADS):
        rows = pl.ds(n, tm, stride=N_KV_HEADS)
        k_ref[rows, :] = kv[:, n * HEAD_DIM:(n + 1) * HEAD_DIM]
        v_ref[rows, :] = kv[:, KV_DIM + n * HEAD_DIM:KV_DIM + (n + 1) * HEAD_DIM]
    kb_ref[...] = kv[:, :KV_DIM].astype(bf16)
    vb_ref[...] = kv[:, KV_DIM:].astype(bf16)
    qi_ref[...] = _dot(xn, wqi_ref[...]).astype(bf16)
    kie_ref[...] = _dot(xn, wke_ref[...]).astype(bf16)
    kio_ref[...] = _dot(xn, wko_ref[...]).astype(bf16)
    kw = _dot(xn, wkw_ref[...])
    lane = lax.broadcasted_iota(i32, kw.shape, 1)
    kw_ref[...] = jnp.where(lane >= IDX_DIM, kw * WI_SCALE, kw)
    u_ref[...] = _dot(xn, wu_ref[...])
    ga_ref[...] = _dot(xn, wga_ref[...])
    gb_ref[...] = _dot(xn, wgb_ref[...])


def _split_w_in(w_in):
    sizes = (N_HEADS * HEAD_DIM, KV_DIM, KV_DIM, IDX_HEADS * IDX_DIM, IDX_DIM, IDX_HEADS,
             D_RNN, D_MODEL, D_MODEL)
    parts, acc = [], 0
    for s in sizes:
        parts.append(w_in[:, acc:acc + s])
        acc += s
    wq, wk, wv, wqi, wki, wwi, wu, wga, wgb = parts
    zk = jnp.zeros_like(wki)
    wkv = jnp.concatenate([wk, wv], axis=1)
    wke = jnp.concatenate([wki, zk], axis=1)
    wko = jnp.concatenate([zk, wki], axis=1)
    wkw = jnp.concatenate(
        [wki, wwi, jnp.zeros((w_in.shape[0], LANES - IDX_DIM - IDX_HEADS), w_in.dtype)], axis=1)
    return tuple(w.astype(bf16) for w in (wq, wkv, wqi, wke, wko, wkw, wu, wga, wgb))


def _proj(x2, g_mix, w_parts):
    n = x2.shape[0]
    tm = min(512, n)
    assert n % tm == 0

    def rows(width):
        return pl.BlockSpec((tm, width), lambda i: (i, 0))

    outs = ((1, N_HEADS * HEAD_DIM, bf16), (N_KV_HEADS, HEAD_DIM, f32), (N_KV_HEADS, HEAD_DIM, f32),
            (1, KV_DIM, bf16), (1, KV_DIM, bf16), (1, IDX_HEADS * IDX_DIM, bf16), (1, LANES, bf16),
            (1, LANES, bf16), (1, LANES, f32), (1, D_RNN, f32), (1, D_MODEL, f32), (1, D_MODEL, f32))
    return pl.pallas_call(
        _proj_kernel,
        grid=(n // tm,),
        in_specs=[rows(D_MODEL), _const_spec((1, D_MODEL))] + [_const_spec(w.shape) for w in w_parts],
        out_specs=[pl.BlockSpec((r * tm, w), lambda i: (i, 0)) for r, w, _ in outs],
        out_shape=[jax.ShapeDtypeStruct((r * n, w), d) for r, w, d in outs],
        compiler_params=pltpu.CompilerParams(
            dimension_semantics=("arbitrary",), vmem_limit_bytes=VMEM_LIMIT),
        name="proj",
    )(x2, g_mix.reshape(1, D_MODEL), *w_parts)


def _greedy_bits(count_at, n_bits, n_start, k):
    def body(it, carry):
        x, n_at = carry
        cand = x | lax.shift_left(jnp.int32(1), jnp.int32(n_bits - 1) - it)
        n = count_at(cand)
        ok = n >= k
        return jnp.where(ok, cand, x), jnp.where(ok, n, n_at)

    return lax.fori_loop(0, n_bits, body, (jnp.zeros(n_start.shape, i32), n_start))


def _kth_largest(count_ge, n_all, k):
    t_u, n_at = _greedy_bits(lambda u: count_ge(_key_to_float(u ^ _INT_MIN)), 32, n_all, k)
    return t_u ^ _INT_MIN, n_at


def _tie_cut(count_eq_below, shape, need, n_bits):
    def body(it, x):
        bit = lax.shift_left(jnp.int32(1), jnp.int32(n_bits - 1) - it)
        cand = x | bit
        cnt = count_eq_below(cand)
        return jnp.where(cnt < need, cand, x)

    return lax.fori_loop(0, n_bits, body, jnp.zeros(shape, i32))


def _prompt_attn_kernel(rb_ref, q_ref, qi_ref, kw_ref, kb_ref, vb_ref, kie_ref, kio_ref, o_ref,
                        score_ref, floor_ref, mask_ref, bias_ref, mx_ref, acc_ref, knorm_ref,
                        *, tq, topk, seq):
    b = pl.program_id(0)
    i = pl.program_id(1)
    tk = tq
    grid = jnp.bfloat16
    krow = lax.broadcasted_iota(i32, (tk, tq), 0)
    qcol = lax.broadcasted_iota(i32, (tk, tq), 1)

    @pl.when((b == 0) & (i == 0))
    def _():
        for t in range(2):
            bucket = _bucket(t * tq + krow - qcol)
            for h in range(N_HEADS):
                bias_ref[t * N_HEADS + h] = jnp.zeros((tq, tk), f32)

            def fill(bb, carry, t=t, bucket=bucket):
                hit = bucket == bb
                for h in range(N_HEADS):
                    bias_ref[t * N_HEADS + h] = jnp.where(
                        hit, rb_ref[bb, h] * LOG2E, bias_ref[t * N_HEADS + h])
                return carry

            lax.fori_loop(0, N_BUCKETS, fill, 0)

    w_t = kw_ref[...].T

    def score_block(j, causal):
        off = pl.multiple_of(j * tk, tk)
        ke = kie_ref[pl.ds(off, tk), :]
        ko = kio_ref[pl.ds(off, tk), :]
        s = jnp.zeros((tk, tq), f32)
        for p in range(IDX_HEADS // 2):
            rhs = qi_ref[:, p * LANES:(p + 1) * LANES]
            c0 = IDX_DIM + 2 * p
            s = s + jnp.maximum(_dot_nt(ke, rhs), 0.0) * w_t[c0:c0 + 1, :]
            s = s + jnp.maximum(_dot_nt(ko, rhs), 0.0) * w_t[c0 + 1:c0 + 2, :]
        if causal:
            s = jnp.where(krow <= qcol, s, NEG_INF)
        score_ref[pl.ds(off, tk), :] = s
        near = s.astype(grid)
        bits = pltpu.bitcast(near, jnp.int16)
        down = pltpu.bitcast(jnp.where(near > 0, bits - 1, bits + 1), grid)
        floor_ref[pl.ds(off, tk), :] = jnp.where(near.astype(f32) > s, down, near)

    def several_per_trip(n, fn, width=2):
        n = jnp.maximum(n, 0)
        done = jnp.int32(0)
        while width >= 1:
            trips = (n - done) // width

            def trip(p, carry, width=width, done=done):
                for w in range(width):
                    fn(done + width * p + w)
                return carry

            lax.fori_loop(0, trips, trip, 0)
            done = done + trips * width
            width //= 2

    several_per_trip(i, lambda j: score_block(j, False), width=SCORE_BLOCKS_PER_TRIP)
    score_block(i, True)

    def floor_count(t):
        t = t.astype(grid)

        def body(j, c):
            off = pl.multiple_of(j * tk, tk)
            hit = jnp.where(floor_ref[pl.ds(off, tk), :] >= t, jnp.ones((), jnp.int16), jnp.zeros((), jnp.int16))
            rows = [hit[g * BF16_SUBLANES:(g + 1) * BF16_SUBLANES] for g in range(tk // BF16_SUBLANES)]
            while len(rows) > 1:
                rows = [a + b for a, b in zip(rows[0::2], rows[1::2])]
            return c + rows[0]

        c = lax.fori_loop(0, i + 1, body, jnp.zeros((BF16_SUBLANES, tq), jnp.int16))
        return jnp.sum(c.astype(i32), axis=0, keepdims=True)

    def score_count(pred):
        def body(j, c):
            off = pl.multiple_of(j * tk, tk)
            hit = pred(score_ref[pl.ds(off, tk), :], off).astype(i32)
            rows = [hit[g * SUBLANES:(g + 1) * SUBLANES] for g in range(tk // SUBLANES)]
            while len(rows) > 1:
                rows = [a + b for a, b in zip(rows[0::2], rows[1::2])]
            return c + rows[0]

        c = lax.fori_loop(0, i + 1, body, jnp.zeros((SUBLANES, tq), i32))
        return jnp.sum(c, axis=0, keepdims=True)

    def count_ge(t):
        return score_count(lambda sc, off: sc >= t)

    n_all = jnp.full((1, tq), (i + 1) * tk, i32)
    lo_key, n_lo = _greedy_bits(lambda key: floor_count(_grid_to_float(key - 2 ** 15)), 16, n_all, topk)
    lo = _grid_to_float(lo_key - 2 ** 15)
    hi = _grid_to_float(lo_key - 2 ** 15 + 1)
    width = hi - lo
    step = jnp.where(width < jnp.inf, width, 0.0) * 2.0 ** -16
    pos, n_ge = _greedy_bits(lambda j: count_ge(lo + j.astype(f32) * step), 16, n_lo, topk)
    thr = lo + pos.astype(f32) * step

    @pl.when(jnp.max(n_ge) > topk)
    def _():
        need = topk - score_count(lambda sc, off: sc > thr)
        cut = _tie_cut(lambda x: score_count(lambda sc, off: (sc == thr) & (krow + off < x)),
                       (1, tq), need, (seq - 1).bit_length())

        def demote(j, carry):
            off = pl.multiple_of(j * tk, tk)
            sc = score_ref[pl.ds(off, tk), :]
            score_ref[pl.ds(off, tk), :] = jnp.where((sc == thr) & (krow + off > cut), -jnp.inf, sc)
            return carry

        lax.fori_loop(0, i + 1, demote, 0)

    far_bias = [rb_ref[N_BUCKETS - 1, h] * LOG2E for h in range(N_HEADS)]

    def to_mask(j, causal):
        off = pl.multiple_of(j * tk, tk)
        sel = score_ref[pl.ds(off, tk), :] >= thr
        if causal:
            sel = sel & (krow <= qcol)
        mask_ref[:, pl.ds(off, tk)] = jnp.where(sel, 0.0, NEG_INF).T

    several_per_trip(i, lambda j: to_mask(j, False))
    to_mask(i, True)

    def group_logits(j, mode):
        off = pl.multiple_of(j * tk, tk)
        mask = mask_ref[:, pl.ds(off, tk)]
        for n in range(N_KV_HEADS):
            ns = slice(n * HEAD_DIM, (n + 1) * HEAD_DIM)
            heads = range(n * GROUP, (n + 1) * GROUP)
            qs = jnp.concatenate([q_ref[:, h * HEAD_DIM:(h + 1) * HEAD_DIM] for h in heads], axis=0)
            s = _dot_nt(qs, kb_ref[pl.ds(off, tk), ns])
            per_head = []
            for g, h in enumerate(heads):
                sg = s[g * tq:(g + 1) * tq] + mask
                if mode != 0:
                    sg = sg + bias_ref[(2 - mode) * N_HEADS + h]
                per_head.append((h, sg))
            yield n, off, ns, per_head

    def over_blocks(block_fn):
        several_per_trip(i - 1, lambda j: block_fn(j, 0), width=ATTN_BLOCKS_PER_TRIP)

        @pl.when(i >= 1)
        def _():
            block_fn(i - 1, 1)
            block_fn(i, 2)

        @pl.when(i == 0)
        def _():
            block_fn(i, 2)

    def exact_max():
        mx_ref[...] = jnp.full(mx_ref.shape, NEG_INF, f32)

        def max_block(j, mode):
            for _, _, _, per_head in group_logits(j, mode):
                for h, sg in per_head:
                    part = sg[:, 0:LANES]
                    for c in range(1, tk // LANES):
                        part = jnp.maximum(part, sg[:, c * LANES:(c + 1) * LANES])
                    if mode == 0:
                        part = part + far_bias[h]
                    mx_ref[h] = jnp.maximum(mx_ref[h], part)

        over_blocks(max_block)
        for h in range(N_HEADS):
            mx_ref[h] = jnp.broadcast_to(jnp.max(mx_ref[h], axis=-1, keepdims=True), (tq, LANES))

    ones_cols = jnp.ones((tk, HEAD_DIM), bf16)

    def value_sweep():
        acc_ref[...] = jnp.zeros(acc_ref.shape, f32)

        def pv_block(j, mode):
            for n, off, ns, per_head in group_logits(j, mode):
                ps = []
                for h, sg in per_head:
                    shift = mx_ref[h] - far_bias[h] if mode == 0 else mx_ref[h]
                    shift = jnp.concatenate([shift] * (tk // LANES), axis=1)
                    ps.append(jnp.exp2(sg - shift).astype(bf16))
                values = jnp.concatenate([vb_ref[pl.ds(off, tk), ns], ones_cols], axis=1)
                acc_ref[n] = acc_ref[n] + _dot(jnp.concatenate(ps, axis=0), values)

        over_blocks(pv_block)

    @pl.when(i == 0)
    def _():
        knorm_ref[...] = jnp.zeros(knorm_ref.shape, f32)

    diag = pl.multiple_of(i * tk, tk)
    for n in range(N_KV_HEADS):
        kf = kb_ref[pl.ds(diag, tk), n * HEAD_DIM:(n + 1) * HEAD_DIM].astype(f32)
        ksq = jnp.max(jnp.sum(kf * kf, axis=-1, keepdims=True), axis=0, keepdims=True)
        knorm_ref[n] = jnp.maximum(knorm_ref[n], jnp.broadcast_to(ksq, knorm_ref.shape[1:]))
    for h in range(N_HEADS):
        qf = q_ref[:, h * HEAD_DIM:(h + 1) * HEAD_DIM].astype(f32)
        qsq = jnp.sum(qf * qf, axis=-1, keepdims=True)
        bias_max = lax.fori_loop(1, N_BUCKETS, lambda bb, m, h=h: jnp.maximum(m, rb_ref[bb, h]),
                                 rb_ref[0, h]) * LOG2E
        bound = jnp.sqrt(qsq * knorm_ref[h // GROUP][0:1, 0:1]) * BOUND_SLACK + bias_max
        mx_ref[h] = jnp.broadcast_to(bound, (tq, LANES))
    value_sweep()

    @pl.when(jnp.min(acc_ref[:, :, HEAD_DIM:]) < MIN_DENOMINATOR)
    def _():
        exact_max()
        value_sweep()

    for h in range(N_HEADS):
        acc = acc_ref[h // GROUP, (h % GROUP) * tq:(h % GROUP + 1) * tq, :]
        out = acc[:, :HEAD_DIM] / acc[:, HEAD_DIM:]
        o_ref[:, h * HEAD_DIM:(h + 1) * HEAD_DIM] = out.astype(o_ref.dtype)


def _prompt_attn(rel_bias, q, qi, kw, kb, vb, kie, kio, batch, seq):
    tq = min(256, seq)
    assert seq % tq == 0 and tq % LANES == 0 and tq >= MAX_DISTANCE
    nq = seq // tq
    topk = min(TOPK_MAX, seq // 4)

    def qrows(width):
        return pl.BlockSpec((tq, width), lambda b, i: (b * nq + i, 0))

    def krows(width):
        return pl.BlockSpec((seq, width), lambda b, i: (b, 0))

    kern = functools.partial(_prompt_attn_kernel, tq=tq, topk=topk, seq=seq)
    return pl.pallas_call(
        kern,
        grid=(batch, nq),
        in_specs=[_smem_spec(), qrows(N_HEADS * HEAD_DIM), qrows(IDX_HEADS * IDX_DIM), qrows(LANES),
                  krows(KV_DIM), krows(KV_DIM), krows(LANES), krows(LANES)],
        out_specs=qrows(N_HEADS * HEAD_DIM),
        out_shape=jax.ShapeDtypeStruct((batch * seq, N_HEADS * HEAD_DIM), bf16),
        scratch_shapes=[
            pltpu.VMEM((seq, tq), f32),
            pltpu.VMEM((seq, tq), jnp.bfloat16),
            pltpu.VMEM((tq, seq), f32),
            pltpu.VMEM((2 * N_HEADS, tq, tq), f32),
            pltpu.VMEM((N_HEADS, tq, LANES), f32),
            pltpu.VMEM((N_KV_HEADS, GROUP * tq, 2 * HEAD_DIM), f32),
            pltpu.VMEM((N_KV_HEADS, SUBLANES, LANES), f32),
        ],
        compiler_params=pltpu.CompilerParams(
            dimension_semantics=("arbitrary", "arbitrary"), vmem_limit_bytes=VMEM_LIMIT),
        name="prompt_attn",
    )(rel_bias, q, qi, kw, kb, vb, kie, kio)


def _fetch_pages(pt_ref, pairs, sem_ref, pages_per_step):
    n_steps = pl.num_programs(1)
    t = pl.program_id(0) * n_steps + pl.program_id(1)
    slot = t % 2

    def copies(step, slot_):
        seq, chunk = step // n_steps, step % n_steps
        for pp in range(pages_per_step):
            page = pt_ref[seq, chunk * pages_per_step + pp]
            for a, (hbm_ref, buf_ref) in enumerate(pairs):
                yield pltpu.make_async_copy(hbm_ref.at[page], buf_ref.at[slot_, pp], sem_ref.at[a, slot_, pp])

    @pl.when(t == 0)
    def _():
        for cp in copies(t, slot):
            cp.start()

    @pl.when(t + 1 < pl.num_programs(0) * n_steps)
    def _():
        for cp in copies(t + 1, 1 - slot):
            cp.start()

    for cp in copies(t, slot):
        cp.wait()
    return slot


def _sample_index_kernel(pt_ref, qh_ref, wc_ref, kin_ref, cki_hbm, past_ref, new_ref, buf_ref, sem_ref,
                         *, pages_per_step, n_new):
    slot = _fetch_pages(pt_ref, [(cki_hbm, buf_ref)], sem_ref, pages_per_step)
    c = pl.program_id(1)
    qh = qh_ref[...]
    wc = wc_ref[...]

    def scores(keys_t):
        s = jnp.maximum(_dot(qh, keys_t), 0.0) * wc
        out = s[0:n_new]
        for h in range(1, IDX_HEADS):
            out = out + s[h * n_new:(h + 1) * n_new]
        return out

    pages = jnp.concatenate([buf_ref[slot, pp].astype(bf16) for pp in range(pages_per_step)], axis=1)
    past_ref[...] = scores(pages)

    @pl.when(c == 0)
    def _():
        sc = scores(kin_ref[...])
        tok = lax.broadcasted_iota(i32, sc.shape, 0)
        key = lax.broadcasted_iota(i32, sc.shape, 1)
        new_ref[...] = jnp.where(key <= tok, sc, NEG_INF)


def _sample_attn_kernel(pt_ref, rb_ref, past_ref, new_ref, q_ref, kn_ref, vn_ref, ck_hbm, cv_hbm, o_ref,
                        score_ref, kcat_ref, vcat_ref, thr_ref, m_ref, l_ref, acc_ref,
                        kbuf_ref, vbuf_ref, sem_ref, *, pages_per_step, n_new, n_past, topk):
    P = pages_per_step
    slot = _fetch_pages(pt_ref, [(ck_hbm, kbuf_ref), (cv_hbm, vbuf_ref)], sem_ref, P)
    c = pl.program_id(1)
    n_steps = pl.num_programs(1)
    n_rows = N_HEADS * n_new
    half = GROUP * n_new
    step_keys = P * PAGE_SIZE
    n_keys = n_past + LANES

    def head_column(bucket_row):
        r = lax.broadcasted_iota(i32, (n_rows, 1), 0) // n_new
        out = jnp.zeros((n_rows, 1), f32)
        for h in range(N_HEADS):
            out = jnp.where(r == h, rb_ref[bucket_row, h] * LOG2E, out)
        return out

    def bias_tile(dist):
        bucket = _bucket(dist)

        def fill(bb, acc):
            return jnp.where(bucket == bb, head_column(bb), acc)

        return lax.fori_loop(0, N_BUCKETS, fill, jnp.zeros(dist.shape, f32))

    @pl.when(c == 0)
    def _():
        score_ref[:, :n_past] = past_ref[...]
        score_ref[:, n_past:] = new_ref[...]

        def count(pred):
            return jnp.sum(pred(score_ref[...]).astype(i32), axis=-1, keepdims=True)

        def count_ge(t):
            return count(lambda sc: sc >= t)

        thr_key, n_ge = _kth_largest(count_ge, jnp.full((n_new, 1), n_keys, i32), topk)
        thr = _key_to_float(thr_key)
        thr_ref[...] = thr

        @pl.when(jnp.max(n_ge) > topk)
        def _():
            need = topk - count_ge(_key_to_float(thr_key + 1))
            colk = lax.broadcasted_iota(i32, (n_new, n_keys), 1)
            cut = _tie_cut(lambda x: count(lambda sc: (sc == thr) & (colk < x)),
                           (n_new, 1), need, (n_keys - 1).bit_length())
            sc = score_ref[...]
            score_ref[...] = jnp.where((sc == thr) & (colk > cut), _key_to_float(thr_key - 1), sc)

        m_ref[...] = jnp.full(m_ref.shape, M_INIT, f32)
        l_ref[...] = jnp.zeros(l_ref.shape, f32)
        acc_ref[...] = jnp.zeros(acc_ref.shape, f32)

    thr = thr_ref[...]
    far_bias = head_column(N_BUCKETS - 1)
    tok = lax.broadcasted_iota(i32, (n_rows, LANES), 0) % n_new
    kpos = lax.broadcasted_iota(i32, (n_rows, LANES), 1)

    def online_update(n, s, values):
        rows = slice(n * half, (n + 1) * half)
        m_old = m_ref[rows]
        m_new = jnp.maximum(m_old, jnp.max(s, axis=-1, keepdims=True))
        alpha = jnp.exp2(m_old - m_new)
        p = jnp.exp2(s - m_new)
        l_ref[rows] = alpha * l_ref[rows] + jnp.sum(p, axis=-1, keepdims=True)
        acc_ref[rows] = alpha * acc_ref[rows] + _dot(p.astype(bf16), values)
        m_ref[rows] = m_new

    def tiled_mask(scores):
        mask = jnp.where(scores >= thr, 0.0, NEG_INF)
        return jnp.concatenate([mask] * GROUP, axis=0)

    for pp in range(P):
        for n in range(N_KV_HEADS):
            rows = pl.ds(n, PAGE_SIZE, stride=N_KV_HEADS)
            kcat_ref[n, pp * PAGE_SIZE:(pp + 1) * PAGE_SIZE, :] = kbuf_ref[slot, pp, rows, :].astype(bf16)
            vcat_ref[n, pp * PAGE_SIZE:(pp + 1) * PAGE_SIZE, :] = vbuf_ref[slot, pp, rows, :].astype(bf16)

    base = pl.multiple_of(c * step_keys, step_keys)
    mask = tiled_mask(score_ref[:, pl.ds(base, step_keys)])
    last_bias = lax.cond(c == n_steps - 1,
                         lambda: bias_tile(PAGE_SIZE + tok - kpos),
                         lambda: jnp.broadcast_to(far_bias, (n_rows, LANES)))
    for n in range(N_KV_HEADS):
        rows = slice(n * half, (n + 1) * half)
        s = _dot_nt(q_ref[rows, :], kcat_ref[n])
        s = jnp.concatenate([s[:, :step_keys - PAGE_SIZE] + far_bias[rows],
                             s[:, step_keys - PAGE_SIZE:] + last_bias[rows]], axis=1)
        online_update(n, s + mask, vcat_ref[n])

    @pl.when(c == n_steps - 1)
    def _():
        causal = (lax.broadcasted_iota(i32, (half, LANES), 1)
                  <= lax.broadcasted_iota(i32, (half, LANES), 0) % n_new)
        mask = jnp.where(causal, tiled_mask(score_ref[:, n_past:]), NEG_INF)
        bias = bias_tile(tok - kpos)
        for n in range(N_KV_HEADS):
            rows = slice(n * half, (n + 1) * half)
            ns = slice(n * HEAD_DIM, (n + 1) * HEAD_DIM)
            s = _dot_nt(q_ref[rows, :], kn_ref[:, ns]) + bias[rows]
            online_update(n, s + mask, vn_ref[:, ns])
        o_ref[...] = acc_ref[...] / l_ref[...]


def _sample_attention(page_table, rel_bias, cache_k, cache_v, cache_kidx, q, qi, kw, kb, vb, n_seq, n_new):
    n_pages = page_table.shape[1]
    n_pool = cache_k.shape[0]
    n_past = n_pages * PAGE_SIZE
    topk = min(TOPK_MAX, (n_past + n_new) // 4)
    P = min(32, n_pages)
    assert n_pages % P == 0 and n_new == SUBLANES
    n_steps = n_pages // P
    n_rows = N_HEADS * n_new
    page_cols = N_KV_HEADS * PAGE_SIZE

    qh = qi.reshape(n_seq, n_new, IDX_HEADS, IDX_DIM).transpose(0, 2, 1, 3).reshape(n_seq, n_rows, IDX_DIM)
    wc = kw[:, IDX_DIM:IDX_DIM + IDX_HEADS].reshape(n_seq, n_new, IDX_HEADS).transpose(0, 2, 1)
    wc = wc.reshape(n_seq, n_rows, 1)
    kin = kw[:, :IDX_DIM].astype(bf16).reshape(n_seq, n_new, IDX_DIM).transpose(0, 2, 1)
    kin = jnp.pad(kin, ((0, 0), (0, 0), (0, LANES - n_new)))
    qs = q.reshape(n_seq, n_new, N_HEADS, HEAD_DIM).transpose(0, 2, 1, 3).reshape(n_seq, n_rows, HEAD_DIM)
    pad = ((0, 0), (0, LANES - n_new), (0, 0))
    kn = jnp.pad(kb.reshape(n_seq, n_new, KV_DIM), pad)
    vn = jnp.pad(vb.reshape(n_seq, n_new, KV_DIM), pad)
    ck = cache_k.reshape(n_pool, page_cols, HEAD_DIM)
    cv = cache_v.reshape(n_pool, page_cols, HEAD_DIM)
    cki = jnp.swapaxes(cache_kidx, -1, -2)

    def seq_block(shape):
        nd = len(shape)
        return pl.BlockSpec((None,) + shape, lambda b, c, pt: (b,) + (0,) * nd)

    paged = pl.BlockSpec(memory_space=pl.ANY)

    past_keys, new_keys = pl.pallas_call(
        functools.partial(_sample_index_kernel, pages_per_step=P, n_new=n_new),
        grid_spec=pltpu.PrefetchScalarGridSpec(
            num_scalar_prefetch=1,
            grid=(n_seq, n_steps),
            in_specs=[seq_block((n_rows, IDX_DIM)), seq_block((n_rows, 1)), seq_block((IDX_DIM, LANES)),
                      paged],
            out_specs=[pl.BlockSpec((None, n_new, P * PAGE_SIZE), lambda b, c, pt: (b, 0, c)),
                       seq_block((n_new, LANES))],
            scratch_shapes=[pltpu.VMEM((2, P, IDX_DIM, PAGE_SIZE), f32),
                            pltpu.SemaphoreType.DMA((1, 2, P))],
        ),
        out_shape=[jax.ShapeDtypeStruct((n_seq, n_new, n_past), f32),
                   jax.ShapeDtypeStruct((n_seq, n_new, LANES), f32)],
        compiler_params=pltpu.CompilerParams(
            dimension_semantics=("arbitrary", "arbitrary"), vmem_limit_bytes=VMEM_LIMIT),
        name="sample_index",
    )(page_table, qh, wc, kin, cki)

    out = pl.pallas_call(
        functools.partial(_sample_attn_kernel, pages_per_step=P, n_new=n_new, n_past=n_past, topk=topk),
        grid_spec=pltpu.PrefetchScalarGridSpec(
            num_scalar_prefetch=1,
            grid=(n_seq, n_steps),
            in_specs=[_smem_spec(), seq_block((n_new, n_past)), seq_block((n_new, LANES)),
                      seq_block((n_rows, HEAD_DIM)), seq_block((LANES, KV_DIM)), seq_block((LANES, KV_DIM)),
                      paged, paged],
            out_specs=seq_block((n_rows, HEAD_DIM)),
            scratch_shapes=[
                pltpu.VMEM((n_new, n_past + LANES), f32),
                pltpu.VMEM((N_KV_HEADS, P * PAGE_SIZE, HEAD_DIM), bf16),
                pltpu.VMEM((N_KV_HEADS, P * PAGE_SIZE, HEAD_DIM), bf16),
                pltpu.VMEM((n_new, 1), f32),
                pltpu.VMEM((n_rows, 1), f32),
                pltpu.VMEM((n_rows, 1), f32),
                pltpu.VMEM((n_rows, HEAD_DIM), f32),
                pltpu.VMEM((2, P, page_cols, HEAD_DIM), f32),
                pltpu.VMEM((2, P, page_cols, HEAD_DIM), f32),
                pltpu.SemaphoreType.DMA((2, 2, P)),
            ],
        ),
        out_shape=jax.ShapeDtypeStruct((n_seq, n_rows, HEAD_DIM), f32),
        compiler_params=pltpu.CompilerParams(
            dimension_semantics=("arbitrary", "arbitrary"), vmem_limit_bytes=VMEM_LIMIT),
        name="sample_attn",
    )(page_table, rel_bias, past_keys, new_keys, qs, kn, vn, ck, cv)

    out = out.reshape(n_seq, N_HEADS, n_new, HEAD_DIM).transpose(0, 2, 1, 3)
    return out.reshape(n_seq * n_new, N_HEADS * HEAD_DIM).astype(bf16)


def _rglru_kernel(u_ref, buf_ref, h0_ref, cw_ref, cb_ref, wg_ref, bg_ref, lam_ref,
                  y_ref, nbuf_ref, hT_ref,
                  ext_ref, a_ref, b_ref, hs_ref, h_ref, *, tt):
    t = pl.program_id(1)
    head = SUBLANES

    @pl.when(t == 0)
    def _():
        ext_ref[head - (CONV_W - 1):head, :] = buf_ref[...]
        h_ref[...] = h0_ref[...]

    ext_ref[head:head + tt, :] = u_ref[...]
    xc = cb_ref[...] + ext_ref[head - 3:head - 3 + tt, :] * cw_ref[0:1, :]
    for j in range(1, CONV_W):
        xc = xc + ext_ref[head - 3 + j:head - 3 + j + tt, :] * cw_ref[j:j + 1, :]
    tail = ext_ref[head + tt - (CONV_W - 1):head + tt, :]
    nbuf_ref[...] = tail
    ext_ref[head - (CONV_W - 1):head, :] = tail

    lam = lam_ref[...]
    neg = -lam
    softplus = jnp.maximum(neg, 0.0) + jnp.log1p(jnp.exp(-jnp.abs(neg)))
    for n in range(LRU_BLOCKS):
        ns = slice(n * LRU_BLOCK_W, (n + 1) * LRU_BLOCK_W)
        xn = xc[:, ns]
        gates = _dot(xn.astype(bf16), wg_ref[n]) + bg_ref[n]
        r = jax.nn.sigmoid(gates[:, :LRU_BLOCK_W])
        ig = jax.nn.sigmoid(gates[:, LRU_BLOCK_W:])
        log_a = (-LRU_C) * r * softplus[:, ns]
        a = jnp.exp(log_a)
        a_ref[:, ns] = a
        b_ref[:, ns] = jnp.sqrt(-jnp.tanh(log_a) * (1.0 + a * a)) * (ig * xn)

    def step(r_, h):
        h = a_ref[pl.ds(r_, 1), :] * h + b_ref[pl.ds(r_, 1), :]
        hs_ref[pl.ds(r_, 1), :] = h
        return h

    h = lax.fori_loop(0, tt, step, h_ref[...], unroll=8)
    h_ref[...] = h
    hT_ref[...] = h
    y_ref[...] = hs_ref[...].astype(y_ref.dtype)


def _rglru(u, conv_buf, h0, conv_w, conv_b, w_rg, b_rg, w_ig, b_ig, lam):
    nb, seq, _ = u.shape
    tt = min(256, seq)
    assert seq % tt == 0 and tt % SUBLANES == 0 and tt >= CONV_W - 1
    wg = jnp.concatenate([w_rg, w_ig], axis=-1).astype(bf16)
    bg = jnp.concatenate([b_rg, b_ig], axis=-1).reshape(LRU_BLOCKS, 1, 2 * LRU_BLOCK_W)

    def per_seq(rows):
        return pl.BlockSpec((None, rows, D_RNN), lambda b, t: (b, 0, 0))

    y, nbuf, hT = pl.pallas_call(
        functools.partial(_rglru_kernel, tt=tt),
        grid=(nb, seq // tt),
        in_specs=[pl.BlockSpec((None, tt, D_RNN), lambda b, t: (b, t, 0)),
                  per_seq(CONV_W - 1), per_seq(1),
                  _const_spec((CONV_W, D_RNN)), _const_spec((1, D_RNN)),
                  _const_spec(wg.shape), _const_spec(bg.shape), _const_spec((1, D_RNN))],
        out_specs=[pl.BlockSpec((None, tt, D_RNN), lambda b, t: (b, t, 0)),
                   per_seq(CONV_W - 1), per_seq(1)],
        out_shape=[jax.ShapeDtypeStruct((nb, seq, D_RNN), bf16),
                   jax.ShapeDtypeStruct((nb, CONV_W - 1, D_RNN), f32),
                   jax.ShapeDtypeStruct((nb, 1, D_RNN), f32)],
        scratch_shapes=[
            pltpu.VMEM((SUBLANES + tt, D_RNN), f32),
            pltpu.VMEM((tt, D_RNN), f32),
            pltpu.VMEM((tt, D_RNN), f32),
            pltpu.VMEM((tt, D_RNN), f32),
            pltpu.VMEM((1, D_RNN), f32),
        ],
        compiler_params=pltpu.CompilerParams(
            dimension_semantics=("arbitrary", "arbitrary"), vmem_limit_bytes=VMEM_LIMIT),
        name="rglru",
    )(u, conv_buf, h0.reshape(nb, 1, D_RNN), conv_w, conv_b.reshape(1, D_RNN), wg, bg,
      lam.reshape(1, D_RNN))
    return y, nbuf, hT.reshape(nb, D_RNN)


def _merge_ffn_kernel(x_ref, attn_ref, lru_ref, ga_ref, gb_ref, woa_ref, wol_ref, wout_ref,
                      gf_ref, wfg_ref, wfu_ref, wfd_ref, gfin_ref, y_ref):
    merged = (jax.nn.sigmoid(ga_ref[...]) * _dot(attn_ref[...], woa_ref[...])
              + jax.nn.sigmoid(gb_ref[...]) * _dot(lru_ref[...], wol_ref[...]))
    h = x_ref[...] + _dot(merged.astype(bf16), wout_ref[...])
    hn = (h * lax.rsqrt(jnp.mean(h * h, axis=-1, keepdims=True) + EPS)) * gf_ref[...]
    hn = hn.astype(bf16)
    act = jax.nn.silu(_dot(hn, wfg_ref[...])) * _dot(hn, wfu_ref[...])
    y = h + _dot(act.astype(bf16), wfd_ref[...])
    y_ref[...] = (y * lax.rsqrt(jnp.mean(y * y, axis=-1, keepdims=True) + EPS)) * gfin_ref[...]


def _merge_ffn(x2, attn, lru, ga, gb, weights, g_ffn, g_final):
    n = x2.shape[0]
    tm = min(256, n)
    assert n % tm == 0
    woa, wol, wout, wfg, wfu, wfd = weights

    def rows(width):
        return pl.BlockSpec((tm, width), lambda i: (i, 0))

    return pl.pallas_call(
        _merge_ffn_kernel,
        grid=(n // tm,),
        in_specs=[rows(D_MODEL), rows(N_HEADS * HEAD_DIM), rows(D_RNN), rows(D_MODEL), rows(D_MODEL),
                  _const_spec(woa.shape), _const_spec(wol.shape), _const_spec(wout.shape),
                  _const_spec((1, D_MODEL)), _const_spec(wfg.shape), _const_spec(wfu.shape),
                  _const_spec(wfd.shape), _const_spec((1, D_MODEL))],
        out_specs=rows(D_MODEL),
        out_shape=jax.ShapeDtypeStruct((n, D_MODEL), f32),
        compiler_params=pltpu.CompilerParams(
            dimension_semantics=("arbitrary",), vmem_limit_bytes=VMEM_LIMIT),
        name="merge_ffn",
    )(x2, attn, lru, ga, gb, woa, wol, wout, g_ffn.reshape(1, D_MODEL), wfg, wfu, wfd,
      g_final.reshape(1, D_MODEL))


def kernel(x_prompt, x_sample, cache_k, cache_v, cache_kidx, state_conv, state_rnn, page_table,
           rel_bias, g_mix, w_in, conv_w, conv_b, w_rgate, b_rgate, w_igate, b_igate, lru_lambda,
           w_o_attn, w_o_lru, w_out, g_ffn, w_ffn_gate, w_ffn_up, w_ffn_down, g_final):
    assert w_in.shape[0] == 1, "one trunk layer"
    batch, seq, _ = x_prompt.shape
    n_seq, n_new, _ = x_sample.shape
    layer = 0

    w_parts = _split_w_in(w_in[layer])
    lru_w = (conv_w[layer], conv_b[layer], w_rgate[layer], b_rgate[layer], w_igate[layer],
             b_igate[layer], lru_lambda[layer])
    out_w = tuple(w[layer].astype(bf16)
                  for w in (w_o_attn, w_o_lru, w_out, w_ffn_gate, w_ffn_up, w_ffn_down))

    xp = x_prompt.reshape(batch * seq, D_MODEL)
    q, k, v, kb, vb, qi, kie, kio, kw, u, ga, gb = _proj(xp, g_mix[layer], w_parts)
    attn = _prompt_attn(rel_bias, q, qi, kw, kb, vb, kie, kio, batch, seq)
    lru, buf_p, h_p = _rglru(u.reshape(batch, seq, D_RNN),
                             jnp.zeros((batch, CONV_W - 1, D_RNN), f32),
                             jnp.zeros((batch, D_RNN), f32), *lru_w)
    y_prompt = _merge_ffn(xp, attn, lru.reshape(batch * seq, D_RNN), ga, gb, out_w,
                          g_ffn[layer], g_final).reshape(batch, seq, D_MODEL)
    new_k_prompt = k.reshape(1, batch, seq, N_KV_HEADS, HEAD_DIM)
    new_v_prompt = v.reshape(1, batch, seq, N_KV_HEADS, HEAD_DIM)
    new_kidx_prompt = kw[:, :IDX_DIM].reshape(1, batch, seq, IDX_DIM)

    xs = x_sample.reshape(n_seq * n_new, D_MODEL)
    q, k, v, kb, vb, qi, kie, kio, kw, u, ga, gb = _proj(xs, g_mix[layer], w_parts)
    attn = _sample_attention(page_table, rel_bias, cache_k[layer], cache_v[layer], cache_kidx[layer],
                             q, qi, kw, kb, vb, n_seq, n_new)
    lru, buf_s, h_s = _rglru(u.reshape(n_seq, n_new, D_RNN), state_conv[layer], state_rnn[layer], *lru_w)
    y_sample = _merge_ffn(xs, attn, lru.reshape(n_seq * n_new, D_RNN), ga, gb, out_w,
                          g_ffn[layer], g_final).reshape(n_seq, n_new, D_MODEL)
    new_k_sample = k.reshape(1, n_seq, n_new, N_KV_HEADS, HEAD_DIM)
    new_v_sample = v.reshape(1, n_seq, n_new, N_KV_HEADS, HEAD_DIM)
    new_kidx_sample = kw[:, :IDX_DIM].reshape(1, n_seq, n_new, IDX_DIM)

    return (y_prompt, y_sample, new_k_prompt, new_v_prompt, new_kidx_prompt, buf_p[None], h_p[None],
            new_k_sample, new_v_sample, new_kidx_sample, buf_s[None], h_s[None])
```
